```python
import jax, jax.numpy as jnp
from jax import lax
import numpy as np

D_MODEL = 1024
BATCH = 8
SEQ = 2048
DEPTH = 1
DEC_BATCH = 8
DEC_SEQ = 8192
PAST_LEN = 128

HG_HEADS = 8
HG_KEY_DIM = D_MODEL // HG_HEADS
HG_VAL_DIM = D_MODEL // HG_HEADS
HG_WIDTH = HG_HEADS * HG_KEY_DIM
HG_VWIDTH = HG_HEADS * HG_VAL_DIM
CHUNK = 64
CONV_WIDTH = D_MODEL
CONV_K = 3
SPLIT_SIZES = (HG_WIDTH, HG_WIDTH, HG_WIDTH, HG_VWIDTH, HG_VWIDTH,
               CONV_WIDTH, CONV_WIDTH, CONV_WIDTH, D_MODEL, D_MODEL)
IN_WIDTH = 3 * HG_WIDTH + 2 * HG_VWIDTH + 3 * CONV_WIDTH + 2 * D_MODEL
N_EXPERTS = 64
TOP_K = 8
N_GROUPS = 8
TOPK_GROUPS = 4
EXPERTS_PER_GROUP = N_EXPERTS // N_GROUPS
EXPERT_HIDDEN = 256
SHARED_HIDDEN = 256
ROUTED_SCALE = 2.5
DISPATCH_BLOCK = 128
EPS = 1e-6

kernel_name = 'hybrid_hgrn2_shortconv_moe_encoder'


def _rmsnorm(x, g):
    xf = x.astype(jnp.float32)
    y = xf * lax.rsqrt(jnp.mean(xf * xf, axis=-1, keepdims=True) + EPS)
    return (y * g.astype(jnp.float32)).astype(x.dtype)


def _chunk_gla(q, k, v, g):
    B, L, H, K = q.shape
    V = v.shape[-1]
    N = L // CHUNK

    def to_chunks(t):
        return t.reshape(B, N, CHUNK, H, t.shape[-1]).transpose(1, 0, 3, 2, 4)

    mask = jnp.tril(jnp.ones((CHUNK, CHUNK), dtype=bool))[:, :, None]

    def step(S, inp):
        qc, kc, vc, gc = inp
        b = jnp.cumsum(gc, axis=2)
        diff = b[:, :, :, None, :] - b[:, :, None, :, :]
        dec = jnp.exp(jnp.where(mask, diff, -jnp.inf))
        att = jnp.einsum('bhtk,bhsk,bhtsk->bhts', qc, kc, dec)
        o_intra = jnp.einsum('bhts,bhsv->bhtv', att, vc)
        o_inter = jnp.einsum('bhtk,bhkv->bhtv', qc * jnp.exp(b), S)
        b_last = b[:, :, -1:, :]
        S_new = jnp.exp(b_last[:, :, 0, :])[..., None] * S + jnp.einsum(
            'bhsk,bhsv->bhkv', kc * jnp.exp(b_last - b), vc)
        return S_new, o_intra + o_inter

    S0 = jnp.zeros((B, H, K, V), jnp.float32)
    _, o = lax.scan(step, S0, (to_chunks(q), to_chunks(k), to_chunks(v), to_chunks(g)))
    return o.transpose(1, 0, 3, 2, 4).reshape(B, L, H, V)


def _hgrn2_direction(qh, vh, f_raw, lb_dir, flip):
    f = lb_dir + (1.0 - lb_dir) * jax.nn.sigmoid(f_raw)
    k = 1.0 - f
    g = jnp.log(f)
    if flip:
        rev = lambda t: jnp.flip(t, axis=1)
        return rev(_chunk_gla(rev(qh), rev(k), rev(vh), rev(g)))
    return _chunk_gla(qh, k, vh, g)


def _token_mixers(h, w_in, lb, hg_norm_w, w_o_hg, conv_w, w_o_conv, w_out):
    Bsz, L, _ = h.shape
    points = np.cumsum(SPLIT_SIZES)[:-1].tolist()
    (q, f_fw, f_bw, iv, og, cb, cc, cx, ga, gb) = jnp.split(h @ w_in, points, axis=-1)

    def heads(t):
        return t.reshape(Bsz, L, HG_HEADS, -1).astype(jnp.float32)

    qh = jax.nn.silu(heads(q))
    vh = heads(iv)
    lb = lb.astype(jnp.float32).reshape(2, HG_HEADS, HG_KEY_DIM)
    o = (_hgrn2_direction(qh, vh, heads(f_fw), lb[0], False)
         + _hgrn2_direction(qh, vh, heads(f_bw), lb[1], True))
    o = o * lax.rsqrt(jnp.mean(o * o, axis=-1, keepdims=True) + EPS)
    o = o * hg_norm_w.astype(jnp.float32) * jax.nn.silu(heads(og))
    y_a = o.reshape(Bsz, L, HG_VWIDTH).astype(h.dtype) @ w_o_hg

    u = cc * cx
    up = jnp.pad(u, ((0, 0), (1, 1), (0, 0)))
    conv = up[:, :-2] * conv_w[0] + up[:, 1:-1] * conv_w[1] + up[:, 2:] * conv_w[2]
    y_b = (cb * conv) @ w_o_conv

    merged = jax.nn.sigmoid(ga) * y_a + jax.nn.sigmoid(gb) * y_b
    return merged @ w_out


def _swiglu(x, w1, w3, w2):
    return (jax.nn.silu(x @ w1) * (x @ w3)) @ w2


def _route(hf, router_w, router_bias):
    T = hf.shape[0]
    scores = jax.nn.sigmoid(hf.astype(jnp.float32) @ router_w.astype(jnp.float32))
    sel = scores + router_bias.astype(jnp.float32)
    grp_score = lax.top_k(sel.reshape(T, N_GROUPS, EXPERTS_PER_GROUP), 2)[0].sum(-1)
    _, gidx = lax.top_k(grp_score, TOPK_GROUPS)
    gmask = jax.nn.one_hot(gidx, N_GROUPS, dtype=jnp.float32).sum(1) > 0
    emask = jnp.repeat(gmask, EXPERTS_PER_GROUP, axis=1)
    _, topi = lax.top_k(jnp.where(emask, sel, -jnp.inf), TOP_K)
    w = jnp.take_along_axis(scores, topi, axis=1)
    w = w / jnp.sum(w, axis=-1, keepdims=True) * ROUTED_SCALE
    return topi, w


def _moe_routed(hf, topi, topw, w1, w3, w2):
    T, D = hf.shape
    A = T * TOP_K
    flat_e = topi.reshape(-1)
    flat_tok = jnp.arange(A, dtype=jnp.int32) // TOP_K
    flat_w = topw.reshape(-1)
    order = jnp.argsort(flat_e)
    se, stok, sw = flat_e[order], flat_tok[order], flat_w[order]
    counts = jnp.bincount(flat_e, length=N_EXPERTS)
    starts = jnp.cumsum(counts) - counts
    pcounts = ((counts + DISPATCH_BLOCK - 1) // DISPATCH_BLOCK) * DISPATCH_BLOCK
    pend = jnp.cumsum(pcounts)
    pstart = pend - pcounts
    dest = pstart[se] + jnp.arange(A) - starts[se]
    NB = -(-A // DISPATCH_BLOCK) + N_EXPERTS
    P = NB * DISPATCH_BLOCK
    row_tok = jnp.full((P,), T, jnp.int32).at[dest].set(stok)
    row_w = jnp.zeros((P,), jnp.float32).at[dest].set(sw)
    block_e = jnp.clip(jnp.searchsorted(pend, jnp.arange(NB) * DISPATCH_BLOCK, side='right'),
                       0, N_EXPERTS - 1)
    hpad = jnp.concatenate([hf, jnp.zeros((1, D), hf.dtype)], axis=0)

    def step(y, inp):
        tok, w, e = inp
        out = _swiglu(hpad[tok], w1[e], w3[e], w2[e])
        return y.at[tok].add(out.astype(jnp.float32) * w[:, None]), None

    y, _ = lax.scan(step, jnp.zeros((T + 1, D), jnp.float32),
                    (row_tok.reshape(NB, DISPATCH_BLOCK), row_w.reshape(NB, DISPATCH_BLOCK), block_e))
    return y[:T].astype(hf.dtype)


def _layer(x, c, w_ada, b_ada, norm1_g, w_in, lb, hg_norm_w, w_o_hg, conv_w, w_o_conv, w_out,
           norm2_g, router_w, router_bias, exp_w1, exp_w3, exp_w2, sh_w1, sh_w3, sh_w2):
    mod = (jax.nn.silu(c) @ w_ada + b_ada)[:, None, :]
    shift1, scale1, gate1, shift2, scale2, gate2 = jnp.split(mod, 6, axis=-1)
    h = _rmsnorm(x, norm1_g) * (1.0 + scale1) + shift1
    x = x + gate1 * _token_mixers(h, w_in, lb, hg_norm_w, w_o_hg, conv_w, w_o_conv, w_out)
    h = _rmsnorm(x, norm2_g) * (1.0 + scale2) + shift2
    hf = h.reshape(-1, D_MODEL)
    topi, topw = _route(hf, router_w, router_bias)
    y = _swiglu(hf, sh_w1, sh_w3, sh_w2) + _moe_routed(hf, topi, topw, exp_w1, exp_w3, exp_w2)
    return x + gate2 * y.reshape(x.shape)


def setup_inputs(seed: int = 0) -> dict:
    key = jax.random.key(seed)
    ks = jax.random.split(key, 24)
    D, F, Fs, E = D_MODEL, EXPERT_HIDDEN, SHARED_HIDDEN, N_EXPERTS
    nrm = lambda k, shape, s: jax.random.normal(k, shape, jnp.float32) * s
    return {
        'x_prompt': nrm(ks[0], (BATCH, SEQ, D), 1.0),
        'x_sample': nrm(ks[1], (DEC_BATCH, DEC_SEQ, D), 1.0),
        'c_prompt': nrm(ks[2], (BATCH, D), 1.0),
        'c_sample': nrm(ks[3], (DEC_BATCH, D), 1.0),
        'w_ada': nrm(ks[4], (DEPTH, D, 6 * D), 0.5 * D ** -0.5),
        'b_ada': nrm(ks[5], (DEPTH, 6 * D), 0.02),
        'norm1_g': 1.0 + nrm(ks[6], (DEPTH, D), 0.02),
        'w_in': nrm(ks[7], (DEPTH, D, IN_WIDTH), D ** -0.5),
        'lower_bounds': nrm(ks[8], (DEPTH + 1, 2, HG_WIDTH), 0.1),
        'hg_norm_w': 1.0 + nrm(ks[9], (DEPTH, HG_VAL_DIM), 0.02),
        'w_o_hg': nrm(ks[10], (DEPTH, HG_VWIDTH, D), HG_VWIDTH ** -0.5),
        'conv_w': nrm(ks[11], (DEPTH, CONV_K, CONV_WIDTH), CONV_K ** -0.5),
        'w_o_conv': nrm(ks[12], (DEPTH, CONV_WIDTH, D), CONV_WIDTH ** -0.5),
        'w_out': nrm(ks[13], (DEPTH, D, D), D ** -0.5),
        'norm2_g': 1.0 + nrm(ks[14], (DEPTH, D), 0.02),
        'router_w': nrm(ks[15], (DEPTH, D, E), D ** -0.5),
        'router_bias': nrm(ks[16], (DEPTH, E), 0.01),
        'exp_w1': nrm(ks[17], (DEPTH, E, D, F), D ** -0.5),
        'exp_w3': nrm(ks[18], (DEPTH, E, D, F), D ** -0.5),
        'exp_w2': nrm(ks[19], (DEPTH, E, F, D), F ** -0.5),
        'sh_w1': nrm(ks[20], (DEPTH, D, Fs), D ** -0.5),
        'sh_w3': nrm(ks[21], (DEPTH, D, Fs), D ** -0.5),
        'sh_w2': nrm(ks[22], (DEPTH, Fs, D), Fs ** -0.5),
        'final_g': 1.0 + nrm(ks[23], (D,), 0.02),
    }


def reference(x_prompt, x_sample, c_prompt, c_sample, w_ada, b_ada, norm1_g, w_in, lower_bounds,
              hg_norm_w, w_o_hg, conv_w, w_o_conv, w_out, norm2_g, router_w, router_bias,
              exp_w1, exp_w3, exp_w2, sh_w1, sh_w3, sh_w2, final_g):
    lb_all = jnp.cumsum(jax.nn.softmax(lower_bounds.astype(jnp.float32), axis=0), axis=0)

    def encode(x, c):
        for l in range(DEPTH):
            x = _layer(x, c, w_ada[l], b_ada[l], norm1_g[l], w_in[l], lb_all[l], hg_norm_w[l],
                       w_o_hg[l], conv_w[l], w_o_conv[l], w_out[l], norm2_g[l], router_w[l],
                       router_bias[l], exp_w1[l], exp_w3[l], exp_w2[l], sh_w1[l], sh_w3[l], sh_w2[l])
        return _rmsnorm(x, final_g)

    y_prompt = encode(x_prompt, c_prompt)
    y_sample = encode(x_sample, c_sample)
    return (y_prompt, y_sample)
```

```python
import functools

import jax
import jax.numpy as jnp
from jax import lax
from jax.experimental import pallas as pl
from jax.experimental.pallas import tpu as pltpu

F32 = jnp.float32
BF16 = jnp.bfloat16
HIGHEST = lax.Precision.HIGHEST

D = 1024
HEADS = 8
HD = D // HEADS
N_SEG = 10
N_EXP = 64
TOP_K = 8
N_GRP = 8
TOPK_GRP = 4
EXP_PER_GRP = N_EXP // N_GRP
HID = 256
ROUTED_SCALE = 2.5
EPS = 1e-6

CH = 64
NEG = -1e30
EXP_CLAMP = 80.0
SUBLANES = 8
LANES = 128
VMEM_LIMIT = 56 * 1024 * 1024

S_Q, S_GF, S_GB, S_V, S_OG, S_CB, S_U, S_GA, S_GBM = range(9)


def _sigmoid(x):
    return jax.nn.sigmoid(x)


def _silu(x):
    return x * jax.nn.sigmoid(x)


def _cparams(sem):
    return pltpu.CompilerParams(dimension_semantics=sem, vmem_limit_bytes=VMEM_LIMIT)


def _ada_kernel(c_ref, w_ref, b_ref, o_ref):
    s = _silu(c_ref[...])
    o_ref[...] = jnp.dot(s, w_ref[...], precision=HIGHEST, preferred_element_type=F32) + b_ref[...]


def _ada(c, w_ada, b_ada):
    nb = c.shape[0]
    return pl.pallas_call(
        _ada_kernel,
        grid=(6,),
        in_specs=[pl.BlockSpec((nb, D), lambda n: (0, 0)),
                  pl.BlockSpec((D, D), lambda n: (0, n)),
                  pl.BlockSpec((1, D), lambda n: (0, n))],
        out_specs=pl.BlockSpec((nb, D), lambda n: (0, n)),
        out_shape=jax.ShapeDtypeStruct((nb, 6 * D), F32),
        compiler_params=_cparams(("arbitrary",)),
        name="ada",
    )(c, w_ada, b_ada.reshape(1, 6 * D))


def _inproj_kernel(x_ref, mod_ref, g1_ref, lbr_ref, w_ref, o_ref, h_sc, cc_sc):
    n = pl.program_id(1)

    @pl.when(n == 0)
    def _():
        x = x_ref[...]
        ms = jnp.mean(x * x, axis=-1, keepdims=True)
        y = x * lax.rsqrt(ms + EPS) * g1_ref[...]
        shift = mod_ref[0, 0:1, :]
        scale = mod_ref[0, 1:2, :]
        h_sc[...] = (y * (1.0 + scale) + shift).astype(BF16)

    acc = jnp.dot(h_sc[...], w_ref[...], preferred_element_type=F32)

    def forget_log(lb_row):
        l0 = lbr_ref[0, lb_row:lb_row + 1, :]
        l1 = lbr_ref[1, lb_row:lb_row + 1, :]
        m = jnp.maximum(l0, l1)
        e0 = jnp.exp(l0 - m)
        e1 = jnp.exp(l1 - m)
        lb = e0 / (e0 + e1)
        f = lb + (1.0 - lb) * _sigmoid(acc)
        return jnp.log(f)

    @pl.when((n == 0) | (n == 4))
    def _():
        o_ref[...] = _silu(acc).astype(BF16)

    @pl.when(n == 1)
    def _():
        o_ref[...] = forget_log(0).astype(BF16)

    @pl.when(n == 2)
    def _():
        o_ref[...] = forget_log(1).astype(BF16)

    @pl.when((n == 3) | (n == 5))
    def _():
        o_ref[...] = acc.astype(BF16)

    @pl.when(n == 6)
    def _():
        cc_sc[...] = acc

    @pl.when(n == 7)
    def _():
        o_ref[...] = (cc_sc[...] * acc).astype(BF16)

    @pl.when(n >= 8)
    def _():
        o_ref[...] = _sigmoid(acc).astype(BF16)


def _inproj(x2, mod3, g1, lbr, w_in, seq_len, tm):
    t = x2.shape[0]
    tiles_per_seq = seq_len // tm

    def slot(i, n):
        return (jnp.where(n >= 7, n - 1, n), i, 0)

    return pl.pallas_call(
        _inproj_kernel,
        grid=(t // tm, N_SEG),
        in_specs=[pl.BlockSpec((tm, D), lambda i, n: (i, 0)),
                  pl.BlockSpec((1, 6, D), lambda i, n: (i // tiles_per_seq, 0, 0)),
                  pl.BlockSpec((1, D), lambda i, n: (0, 0)),
                  pl.BlockSpec((2, 2, D), lambda i, n: (0, 0, 0)),
                  pl.BlockSpec((D, D), lambda i, n: (0, n))],
        out_specs=pl.BlockSpec((None, tm, D), slot),
        out_shape=jax.ShapeDtypeStruct((9, t, D), BF16),
        scratch_shapes=[pltpu.VMEM((tm, D), BF16), pltpu.VMEM((tm, D), F32)],
        compiler_params=_cparams(("arbitrary", "arbitrary")),
        name="inproj",
    )(x2, mod3, g1, lbr, w_in)


def _hgrn_kernel(*refs, reverse, final, n_chunks):
    if final:
        q_ref, g_ref, v_ref, og_ref, ob_ref, nw_ref, o_ref, st_sc = refs
    else:
        q_ref, g_ref, v_ref, o_ref, st_sc = refs

    @pl.when(pl.program_id(1) == 0)
    def _():
        st_sc[...] = jnp.zeros_like(st_sc)

    def row_of(p):
        return (CH - 1 - p) if reverse else p

    ti = lax.broadcasted_iota(jnp.int32, (CH, CH), 0)
    si = lax.broadcasted_iota(jnp.int32, (CH, CH), 1)
    if reverse:
        pt, ps = CH - 1 - ti, CH - 1 - si
    else:
        pt, ps = ti, si
    causal = ps <= pt
    tri = causal.astype(BF16)
    m1 = ((pt >= 16) & (pt < 32) & (ps < 16)) | ((pt >= 48) & (ps >= 32) & (ps < 48))
    m2 = causal & ((pt // 16) == (ps // 16))

    ri = lax.broadcasted_iota(jnp.int32, (CH, HD), 0)
    pr = (CH - 1 - ri) if reverse else ri
    q0 = pr >= 32
    k0 = pr < 32
    q1a = (pr >= 16) & (pr < 32)
    q1b = pr >= 48
    k1a = pr < 16
    k1b = (pr >= 32) & (pr < 48)
    blk = pr // 16

    def chunk(c, carry):
        cc = (n_chunks - 1 - c) if reverse else c
        rows = pl.ds(pl.multiple_of(cc * CH, CH), CH)
        g_all = g_ref[rows, :]
        b_all = jnp.dot(tri, g_all, preferred_element_type=F32)
        for h in range(HEADS):
            cols = slice(h * HD, (h + 1) * HD)
            q = q_ref[rows, cols].astype(F32)
            v = v_ref[rows, cols]
            g = g_all[:, cols].astype(F32)
            b = b_all[:, cols]
            kk = 1.0 - jnp.exp(g)

            def bro(p):
                return b[row_of(p):row_of(p) + 1, :]

            r0 = bro(31)
            qa = q * jnp.exp(jnp.where(q0, b - r0, NEG))
            ka = kk * jnp.exp(jnp.where(k0, r0 - b, NEG))
            r1a, r1b = bro(15), bro(47)
            qb = q * jnp.exp(jnp.where(q1a, b - r1a, jnp.where(q1b, b - r1b, NEG)))
            kb = kk * jnp.exp(jnp.where(k1a, r1a - b, jnp.where(k1b, r1b - b, NEG)))
            mid = jnp.where(blk == 0, bro(7), jnp.where(blk == 1, bro(23),
                            jnp.where(blk == 2, bro(39), bro(55))))
            qc = q * jnp.exp(jnp.minimum(b - mid, EXP_CLAMP))
            kc = kk * jnp.exp(jnp.minimum(mid - b, EXP_CLAMP))

            def nt(a, bb):
                return lax.dot_general(a.astype(BF16), bb.astype(BF16), (((1,), (1,)), ((), ())),
                                       preferred_element_type=F32)

            att = nt(qa, ka) + jnp.where(m1, nt(qb, kb), 0.0) + jnp.where(m2, nt(qc, kc), 0.0)
            o = jnp.dot(att.astype(BF16), v, preferred_element_type=F32)

            st = st_sc[h]
            o = o + nt(q * jnp.exp(b), st)
            bl = bro(CH - 1)
            kd = (kk * jnp.exp(bl - b)).astype(BF16)
            upd = lax.dot_general(v, kd, (((0,), (0,)), ((), ())), preferred_element_type=F32)
            st_sc[h] = st * jnp.exp(bl) + upd

            if final:
                o = o + ob_ref[rows, cols]
                o = o * lax.rsqrt(jnp.mean(o * o, axis=-1, keepdims=True) + EPS)
                o = o * nw_ref[...] * og_ref[rows, cols].astype(F32)
                o_ref[rows, cols] = o.astype(o_ref.dtype)
            else:
                o_ref[rows, cols] = o
        return carry

    lax.fori_loop(0, n_chunks, chunk, 0)


def _hgrn(proj, o_bw, hg_norm_w, seq_len, ct, reverse, final):
    t = proj.shape[1]
    nt_ = seq_len // ct
    nb = t // seq_len

    def tile(b, j):
        return b * nt_ + ((nt_ - 1 - j) if reverse else j)

    def pspec(slot):
        return pl.BlockSpec((None, ct, D), lambda b, j: (slot, tile(b, j), 0))

    in_specs = [pspec(S_Q), pspec(S_GB if reverse else S_GF), pspec(S_V)]
    args = [proj, proj, proj]
    if final:
        in_specs += [pspec(S_OG), pl.BlockSpec((ct, D), lambda b, j: (tile(b, j), 0)),
                     pl.BlockSpec((1, HD), lambda b, j: (0, 0))]
        args += [proj, o_bw, hg_norm_w]
    return pl.pallas_call(
        functools.partial(_hgrn_kernel, reverse=reverse, final=final, n_chunks=ct // CH),
        grid=(nb, nt_),
        in_specs=in_specs,
        out_specs=pl.BlockSpec((ct, D), lambda b, j: (tile(b, j), 0)),
        out_shape=jax.ShapeDtypeStruct((t, D), BF16 if final else F32),
        scratch_shapes=[pltpu.VMEM((HEADS, HD, HD), F32)],
        compiler_params=_cparams(("arbitrary", "arbitrary")),
        name="hgrn_final" if final else "hgrn_bw",
    )(*args)


def _mix_kernel(og_ref, cb_ref, u_ref, up_ref, un_ref, ga_ref, gb_ref, x_ref, mod_ref, cw_ref,
                wa_ref, wb_ref, wo_ref, g2_ref, s1_ref, s3_ref, s2_ref,
                xsh_ref, h2_ref, *, tiles_per_seq, tm):
    i = pl.program_id(0)
    pos = i % tiles_per_seq
    u = u_ref[...].astype(F32)
    prev_row = jnp.where(pos == 0, 0.0, up_ref[15:16, :].astype(F32))
    next_row = jnp.where(pos == tiles_per_seq - 1, 0.0, un_ref[0:1, :].astype(F32))
    ri = lax.broadcasted_iota(jnp.int32, (tm, D), 0)
    u_prev = jnp.where(ri == 0, prev_row, pltpu.roll(u, 1, axis=0))
    u_next = jnp.where(ri == tm - 1, next_row, pltpu.roll(u, tm - 1, axis=0))
    conv = u_prev * cw_ref[0:1, :] + u * cw_ref[1:2, :] + u_next * cw_ref[2:3, :]
    yb = jnp.dot((cb_ref[...].astype(F32) * conv).astype(BF16), wb_ref[...], preferred_element_type=F32)
    ya = jnp.dot(og_ref[...], wa_ref[...], preferred_element_type=F32)
    merged = ga_ref[...].astype(F32) * ya + gb_ref[...].astype(F32) * yb
    mix = jnp.dot(merged.astype(BF16), wo_ref[...], preferred_element_type=F32)
    gate1 = mod_ref[0, 2:3, :]
    shift2 = mod_ref[0, 3:4, :]
    scale2 = mod_ref[0, 4:5, :]
    gate2 = mod_ref[0, 5:6, :]
    x1 = x_ref[...] + gate1 * mix
    ms = jnp.mean(x1 * x1, axis=-1, keepdims=True)
    h2 = (x1 * lax.rsqrt(ms + EPS) * g2_ref[...]) * (1.0 + scale2) + shift2
    h2_ref[...] = h2
    hb = h2.astype(BF16)
    a = _silu(jnp.dot(hb, s1_ref[...], preferred_element_type=F32)) * jnp.dot(
        hb, s3_ref[...], preferred_element_type=F32)
    shared = jnp.dot(a.astype(BF16), s2_ref[...], preferred_element_type=F32)
    xsh_ref[...] = x1 + gate2 * shared


def _mix(proj, og, x2, mod3, conv_w, w_o_hg, w_o_conv, w_out, g2, s1, s3, s2, seq_len, tm):
    t = x2.shape[0]
    tiles_per_seq = seq_len // tm
    hb = tm // 16
    nhb = t // 16

    def pspec(slot):
        return pl.BlockSpec((None, tm, D), lambda i: (slot, i, 0))

    def full(shape):
        return pl.BlockSpec(shape, lambda i: (0,) * len(shape))

    in_specs = [pl.BlockSpec((tm, D), lambda i: (i, 0)),
                pspec(S_CB), pspec(S_U),
                pl.BlockSpec((None, 16, D), lambda i: (S_U, jnp.maximum(i * hb - 1, 0), 0)),
                pl.BlockSpec((None, 16, D), lambda i: (S_U, jnp.minimum((i + 1) * hb, nhb - 1), 0)),
                pspec(S_GA), pspec(S_GBM),
                pl.BlockSpec((tm, D), lambda i: (i, 0)),
                pl.BlockSpec((1, 6, D), lambda i: (i // tiles_per_seq, 0, 0)),
                full((3, D)), full((D, D)), full((D, D)), full((D, D)), full((1, D)),
                full((D, HID)), full((D, HID)), full((HID, D))]
    return pl.pallas_call(
        functools.partial(_mix_kernel, tiles_per_seq=tiles_per_seq, tm=tm),
        grid=(t // tm,),
        in_specs=in_specs,
        out_specs=[pl.BlockSpec((tm, D), lambda i: (i, 0)), pl.BlockSpec((tm, D), lambda i: (i, 0))],
        out_shape=[jax.ShapeDtypeStruct((t, D), F32), jax.ShapeDtypeStruct((t, D), F32)],
        compiler_params=_cparams(("arbitrary",)),
        name="mix",
    )(og, proj, proj, proj, proj, proj, proj, x2, mod3, conv_w, w_o_hg, w_o_conv, w_out, g2, s1, s3, s2)


def _route_kernel(h_ref, rwt_ref, bias_ref, ustrict_ref, eid_ref, rank_ref, wt_ref, cnt_ref, *, tm):
    @pl.when(pl.program_id(0) == 0)
    def _():
        cnt_ref[...] = jnp.zeros_like(cnt_ref)

    logits = lax.dot_general(rwt_ref[...], h_ref[...], (((1,), (1,)), ((), ())),
                             precision=HIGHEST, preferred_element_type=F32)
    scores = _sigmoid(logits)
    sel = scores + bias_ref[:, 0:1]

    ief = lax.broadcasted_iota(jnp.int32, (N_EXP, tm), 0).astype(F32)
    sel3 = sel.reshape(N_GRP, EXP_PER_GRP, tm)
    i3 = lax.broadcasted_iota(jnp.int32, (N_GRP, EXP_PER_GRP, tm), 1).astype(F32)
    top1 = jnp.max(sel3, axis=1, keepdims=True)
    first = jnp.min(jnp.where(sel3 == top1, i3, float(EXP_PER_GRP)), axis=1, keepdims=True)
    top2 = jnp.max(jnp.where(i3 == first, -jnp.inf, sel3), axis=1, keepdims=True)
    grp = jnp.broadcast_to(top1 + top2, (N_GRP, EXP_PER_GRP, tm)).reshape(N_EXP, tm)
    ig = jnp.floor(ief * (1.0 / EXP_PER_GRP))
    beaten = jnp.zeros((N_EXP, tm), F32)
    for j in range(N_GRP):
        gj = grp[j * EXP_PER_GRP:j * EXP_PER_GRP + 1, :]
        beaten = beaten + jnp.where((gj > grp) | ((gj == grp) & (float(j) < ig)), 1.0, 0.0)
    cand = jnp.where(beaten < float(TOPK_GRP), sel, -jnp.inf)
    chosen = jnp.zeros((N_EXP, tm), F32)
    for _ in range(TOP_K):
        mx = jnp.max(cand, axis=0, keepdims=True)
        idx = jnp.min(jnp.where(cand == mx, ief, float(N_EXP)), axis=0, keepdims=True)
        hit = ief == idx
        chosen = jnp.where(hit, 1.0, chosen)
        cand = jnp.where(hit, -jnp.inf, cand)

    w = scores * chosen
    w = w / jnp.sum(w, axis=0, keepdims=True) * ROUTED_SCALE
    chosen_b = chosen.astype(BF16)
    rank = jnp.dot(chosen_b, ustrict_ref[...], preferred_element_type=F32) + cnt_ref[:, 0:1]
    cnt_ref[...] = cnt_ref[...] + jnp.sum(chosen, axis=1, keepdims=True)
    lstrict = (lax.broadcasted_iota(jnp.int32, (N_EXP, N_EXP), 1)
               < lax.broadcasted_iota(jnp.int32, (N_EXP, N_EXP), 0)).astype(BF16)
    slotpos = jnp.dot(lstrict, chosen_b, preferred_element_type=F32)
    for k in range(TOP_K):
        mk = (chosen > 0.5) & (slotpos == float(k))
        eid_ref[k:k + 1, :] = jnp.sum(jnp.where(mk, ief, 0.0), axis=0, keepdims=True).astype(jnp.int32)
        rank_ref[k:k + 1, :] = jnp.sum(jnp.where(mk, rank, 0.0), axis=0, keepdims=True).astype(jnp.int32)
        wt_ref[k:k + 1, :] = jnp.sum(jnp.where(mk, w, 0.0), axis=0, keepdims=True)


def _route(h2, router_wt, router_bias, tm):
    t = h2.shape[0]
    ustrict = jnp.triu(jnp.ones((tm, tm), BF16), 1)
    bias = jnp.broadcast_to(router_bias.reshape(N_EXP, 1), (N_EXP, LANES))
    kt = lambda i: (0, i)
    return pl.pallas_call(
        functools.partial(_route_kernel, tm=tm),
        grid=(t // tm,),
        in_specs=[pl.BlockSpec((tm, D), lambda i: (i, 0)),
                  pl.BlockSpec((N_EXP, D), lambda i: (0, 0)),
                  pl.BlockSpec((N_EXP, LANES), lambda i: (0, 0)),
                  pl.BlockSpec((tm, tm), lambda i: (0, 0))],
        out_specs=[pl.BlockSpec((TOP_K, tm), kt), pl.BlockSpec((TOP_K, tm), kt),
                   pl.BlockSpec((TOP_K, tm), kt), pl.BlockSpec((N_EXP, LANES), lambda i: (0, 0))],
        out_shape=[jax.ShapeDtypeStruct((TOP_K, t), jnp.int32), jax.ShapeDtypeStruct((TOP_K, t), jnp.int32),
                   jax.ShapeDtypeStruct((TOP_K, t), F32), jax.ShapeDtypeStruct((N_EXP, LANES), F32)],
        compiler_params=_cparams(("arbitrary",)),
        name="route",
    )(h2, router_wt, bias, ustrict)


def _row_tile(ref, row):
    return ref.at[pl.ds(pl.multiple_of(row * SUBLANES, SUBLANES), SUBLANES), :]


def _dispatch_kernel(dest_ref, h_ref, xs_ref, slab, sem, *, tm):
    for j in range(SUBLANES):
        slab[pl.ds(j, tm, stride=SUBLANES), :] = h_ref[:, j * LANES:(j + 1) * LANES]

    def send(t, carry):
        src = _row_tile(slab, t)
        for k in range(TOP_K):
            pltpu.make_async_copy(src, _row_tile(xs_ref, dest_ref[k, t]), sem).start()
        return carry

    lax.fori_loop(0, tm, send, 0)
    for _ in range(TOP_K):
        pltpu.make_async_copy(slab, xs_ref.at[pl.ds(0, tm * SUBLANES), :], sem).wait()


def _dispatch(h2, dest, tm):
    t = h2.shape[0]
    return pl.pallas_call(
        functools.partial(_dispatch_kernel, tm=tm),
        grid=(t // tm,),
        in_specs=[pl.BlockSpec((TOP_K, tm), lambda i: (0, i), memory_space=pltpu.SMEM),
                  pl.BlockSpec((tm, D), lambda i: (i, 0))],
        out_specs=pl.BlockSpec(memory_space=pl.ANY),
        out_shape=jax.ShapeDtypeStruct((t * TOP_K * SUBLANES, LANES), F32),
        scratch_shapes=[pltpu.VMEM((tm * SUBLANES, LANES), F32), pltpu.SemaphoreType.DMA],
        compiler_params=_cparams(("arbitrary",)),
        name="dispatch",
    )(dest, h2)


def _gmm_kernel(blk_ref, eid_ref, lo_ref, hi_ref, first_ref, x_ref, w1_ref, w3_ref, w2_ref, o_ref, *, bm):
    w = pl.program_id(0)
    x = jnp.concatenate([x_ref[pl.ds(j, bm, stride=SUBLANES), :] for j in range(SUBLANES)],
                        axis=1).astype(BF16)
    a = _silu(jnp.dot(x, w1_ref[...], preferred_element_type=F32)) * jnp.dot(
        x, w3_ref[...], preferred_element_type=F32)
    y = jnp.dot(a.astype(BF16), w2_ref[...], preferred_element_type=F32)
    ri = lax.broadcasted_iota(jnp.int32, (bm, LANES), 0)
    mine = (ri >= lo_ref[w]) & (ri < hi_ref[w])

    @pl.when(first_ref[w] == 1)
    def _():
        for j in range(SUBLANES):
            o_ref[pl.ds(j, bm, stride=SUBLANES), :] = jnp.where(mine, y[:, j * LANES:(j + 1) * LANES], 0.0)

    @pl.when(first_ref[w] == 0)
    def _():
        for j in range(SUBLANES):
            rows = pl.ds(j, bm, stride=SUBLANES)
            o_ref[rows, :] = o_ref[rows, :] + jnp.where(mine, y[:, j * LANES:(j + 1) * LANES], 0.0)


def _gmm(xs, meta, w1, w3, w2, bm):
    n_items = meta[0].shape[0]
    grid_spec = pltpu.PrefetchScalarGridSpec(
        num_scalar_prefetch=5,
        grid=(n_items,),
        in_specs=[pl.BlockSpec((bm * SUBLANES, LANES), lambda w, blk, eid, lo, hi, fi: (blk[w], 0)),
                  pl.BlockSpec((None, D, HID), lambda w, blk, eid, lo, hi, fi: (eid[w], 0, 0)),
                  pl.BlockSpec((None, D, HID), lambda w, blk, eid, lo, hi, fi: (eid[w], 0, 0)),
                  pl.BlockSpec((None, HID, D), lambda w, blk, eid, lo, hi, fi: (eid[w], 0, 0))],
        out_specs=pl.BlockSpec((bm * SUBLANES, LANES), lambda w, blk, eid, lo, hi, fi: (blk[w], 0)),
    )
    return pl.pallas_call(
        functools.partial(_gmm_kernel, bm=bm),
        grid_spec=grid_spec,
        out_shape=jax.ShapeDtypeStruct(xs.shape, F32),
        compiler_params=_cparams(("arbitrary",)),
        name="gmm",
    )(*meta, xs, w1, w3, w2)


def _gmm_metadata(counts, n_rows, bm):
    n_blk = n_rows // bm
    n_items = n_blk + N_EXP - 1
    ends = jnp.cumsum(counts)
    starts = ends - counts
    first_blk = starts // bm
    last_blk = jnp.maximum(ends - 1, 0) // bm
    per_e = jnp.where(counts > 0, last_blk - first_blk + 1, 0)
    item_end = jnp.cumsum(per_e)
    item_start = item_end - per_e
    total = item_end[-1]
    w = jnp.arange(n_items, dtype=jnp.int32)
    wc = jnp.minimum(w, total - 1)
    e = jnp.searchsorted(item_end, wc, side="right").astype(jnp.int32)
    blk = (first_blk[e] + wc - item_start[e]).astype(jnp.int32)
    valid = w < total
    lo = jnp.where(valid, jnp.maximum(starts[e], blk * bm) - blk * bm, 0).astype(jnp.int32)
    hi = jnp.where(valid, jnp.minimum(ends[e], (blk + 1) * bm) - blk * bm, 0).astype(jnp.int32)
    first = jnp.concatenate([jnp.ones((1,), jnp.int32), (blk[1:] != blk[:-1]).astype(jnp.int32)])
    return blk, e, lo, hi, first, starts


def _combine_kernel(dest_ref, wt_ref, xsh_ref, mod_ref, fg_ref, ys_ref, o_ref, ybuf, sem, *, tm):
    def fetch(t, carry):
        for k in range(TOP_K):
            pltpu.make_async_copy(_row_tile(ys_ref, dest_ref[k, t]), _row_tile(ybuf, k * tm + t), sem).start()
        return carry

    lax.fori_loop(0, tm, fetch, 0)
    pltpu.make_async_copy(ys_ref.at[pl.ds(0, TOP_K * tm * SUBLANES), :], ybuf, sem).wait()

    routed = jnp.zeros((tm, D), F32)
    for k in range(TOP_K):
        yk = jnp.concatenate(
            [ybuf[pl.ds(k * tm * SUBLANES + j, tm, stride=SUBLANES), :] for j in range(SUBLANES)], axis=1)
        routed = routed + wt_ref[:, k:k + 1] * yk
    gate2 = mod_ref[0, 5:6, :]
    x = xsh_ref[...] + gate2 * routed
    ms = jnp.mean(x * x, axis=-1, keepdims=True)
    o_ref[...] = x * lax.rsqrt(ms + EPS) * fg_ref[...]


def _combine(ys, dest, wts_t, xsh, mod3, final_g, seq_len, tm):
    t = xsh.shape[0]
    tiles_per_seq = seq_len // tm
    return pl.pallas_call(
        functools.partial(_combine_kernel, tm=tm),
        grid=(t // tm,),
        in_specs=[pl.BlockSpec((TOP_K, tm), lambda i: (0, i), memory_space=pltpu.SMEM),
                  pl.BlockSpec((tm, TOP_K), lambda i: (i, 0)),
                  pl.BlockSpec((tm, D), lambda i: (i, 0)),
                  pl.BlockSpec((1, 6, D), lambda i: (i // tiles_per_seq, 0, 0)),
                  pl.BlockSpec((1, D), lambda i: (0, 0)),
                  pl.BlockSpec(memory_space=pl.ANY)],
        out_specs=pl.BlockSpec((tm, D), lambda i: (i, 0)),
        out_shape=jax.ShapeDtypeStruct((t, D), F32),
        scratch_shapes=[pltpu.VMEM((TOP_K * tm * SUBLANES, LANES), F32), pltpu.SemaphoreType.DMA],
        compiler_params=_cparams(("arbitrary",)),
        name="combine",
    )(dest, wts_t, xsh, mod3, final_g, ys)


def _pick(n, cap):
    t = cap
    while n % t:
        t //= 2
    return t


def _encode(x, mod, p):
    nb, seq_len, _ = x.shape
    t = nb * seq_len
    x2 = x.reshape(t, D)
    mod3 = mod.reshape(nb, 6, D)

    proj = _inproj(x2, mod3, p["norm1_g"], p["lower_bounds"], p["w_in"], seq_len, _pick(seq_len, 1024))
    ct = _pick(seq_len, 512)
    o_bw = _hgrn(proj, None, None, seq_len, ct, reverse=True, final=False)
    og = _hgrn(proj, o_bw, p["hg_norm_w"], seq_len, ct, reverse=False, final=True)
    xsh, h2 = _mix(proj, og, x2, mod3, p["conv_w"], p["w_o_hg"], p["w_o_conv"], p["w_out"], p["norm2_g"],
                   p["sh_w1"], p["sh_w3"], p["sh_w2"], seq_len, _pick(seq_len, 512))

    eid, rank, wts, cnt = _route(h2, p["router_wt"], p["router_bias"], _pick(t, 1024))
    counts = cnt[:, 0].astype(jnp.int32)
    bm = 256
    blk, item_e, lo, hi, first, starts = _gmm_metadata(counts, t * TOP_K, bm)
    dest = starts[eid].astype(jnp.int32) + rank
    xs = _dispatch(h2, dest, _pick(t, 512))
    ys = _gmm(xs, (blk, item_e, lo, hi, first), p["exp_w1"], p["exp_w3"], p["exp_w2"], bm)
    out = _combine(ys, dest, wts.T, xsh, mod3, p["final_g"], seq_len, _pick(seq_len, 256))
    return out.reshape(nb, seq_len, D)


def kernel(x_prompt, x_sample, c_prompt, c_sample, w_ada, b_ada, norm1_g, w_in, lower_bounds, hg_norm_w,
           w_o_hg, conv_w, w_o_conv, w_out, norm2_g, router_w, router_bias, exp_w1, exp_w3, exp_w2,
           sh_w1, sh_w3, sh_w2, final_g):
    p = {
        "norm1_g": norm1_g[0].reshape(1, D),
        "lower_bounds": lower_bounds.astype(F32),
        "w_in": w_in[0].astype(BF16),
        "hg_norm_w": hg_norm_w[0].reshape(1, HD),
        "w_o_hg": w_o_hg[0].astype(BF16),
        "conv_w": conv_w[0],
        "w_o_conv": w_o_conv[0].astype(BF16),
        "w_out": w_out[0].astype(BF16),
        "norm2_g": norm2_g[0].reshape(1, D),
        "router_wt": router_w[0].T,
        "router_bias": router_bias[0],
        "exp_w1": exp_w1[0].astype(BF16),
        "exp_w3": exp_w3[0].astype(BF16),
        "exp_w2": exp_w2[0].astype(BF16),
        "sh_w1": sh_w1[0].astype(BF16),
        "sh_w3": sh_w3[0].astype(BF16),
        "sh_w2": sh_w2[0].astype(BF16),
        "final_g": final_g.reshape(1, D),
    }
    nbp = c_prompt.shape[0]
    mod = _ada(jnp.concatenate([c_prompt, c_sample], axis=0), w_ada[0], b_ada[0])
    y_prompt = _encode(x_prompt, mod[:nbp], p)
    y_sample = _encode(x_sample, mod[nbp:], p)
    return (y_prompt, y_sample)
```

```python
import functools

import jax
import jax.numpy as jnp
from jax import lax
from jax.experimental import pallas as pl
from jax.experimental.pallas import tpu as pltpu

F32 = jnp.float32
BF16 = jnp.bfloat16
HIGHEST = lax.Precision.HIGHEST

D = 1024
HEADS = 8
HD = D // HEADS
N_SEG = 10
N_EXP = 64
TOP_K = 8
N_GRP = 8
TOPK_GRP = 4
EXP_PER_GRP = N_EXP // N_GRP
HID = 256
ROUTED_SCALE = 2.5
EPS = 1e-6

CH = 64
NEG = -1e30
EXP_CLAMP = 80.0
SUBLANES = 8
LANES = 128
VMEM_LIMIT = 56 * 1024 * 1024

S_Q, S_GF, S_GB, S_V, S_OG, S_CB, S_U, S_GA, S_GBM = range(9)


def _sigmoid(x):
    return jax.nn.sigmoid(x)


def _silu(x):
    return x * jax.nn.sigmoid(x)


def _cparams(sem):
    return pltpu.CompilerParams(dimension_semantics=sem, vmem_limit_bytes=VMEM_LIMIT)


def _ada_kernel(c_ref, w_ref, b_ref, o_ref):
    s = _silu(c_ref[...])
    o_ref[...] = jnp.dot(s, w_ref[...], precision=HIGHEST, preferred_element_type=F32) + b_ref[...]


def _ada(c, w_ada, b_ada):
    nb = c.shape[0]
    return pl.pallas_call(
        _ada_kernel,
        grid=(6,),
        in_specs=[pl.BlockSpec((nb, D), lambda n: (0, 0)),
                  pl.BlockSpec((D, D), lambda n: (0, n)),
                  pl.BlockSpec((1, D), lambda n: (0, n))],
        out_specs=pl.BlockSpec((nb, D), lambda n: (0, n)),
        out_shape=jax.ShapeDtypeStruct((nb, 6 * D), F32),
        compiler_params=_cparams(("arbitrary",)),
        name="ada",
    )(c, w_ada, b_ada.reshape(1, 6 * D))


def _inproj_kernel(x_ref, mod_ref, g1_ref, lbr_ref, w_ref, o_ref, h_sc, cc_sc):
    n = pl.program_id(1)

    @pl.when(n == 0)
    def _():
        x = x_ref[...]
        ms = jnp.mean(x * x, axis=-1, keepdims=True)
        y = x * lax.rsqrt(ms + EPS) * g1_ref[...]
        shift = mod_ref[0, 0:1, :]
        scale = mod_ref[0, 1:2, :]
        h_sc[...] = (y * (1.0 + scale) + shift).astype(BF16)

    acc = jnp.dot(h_sc[...], w_ref[...], preferred_element_type=F32)

    def forget_log(lb_row):
        l0 = lbr_ref[0, lb_row:lb_row + 1, :]
        l1 = lbr_ref[1, lb_row:lb_row + 1, :]
        m = jnp.maximum(l0, l1)
        e0 = jnp.exp(l0 - m)
        e1 = jnp.exp(l1 - m)
        lb = e0 / (e0 + e1)
        f = lb + (1.0 - lb) * _sigmoid(acc)
        return jnp.log(f)

    @pl.when((n == 0) | (n == 4))
    def _():
        o_ref[...] = _silu(acc).astype(BF16)

    @pl.when(n == 1)
    def _():
        o_ref[...] = forget_log(0).astype(BF16)

    @pl.when(n == 2)
    def _():
        o_ref[...] = forget_log(1).astype(BF16)

    @pl.when((n == 3) | (n == 5))
    def _():
        o_ref[...] = acc.astype(BF16)

    @pl.when(n == 6)
    def _():
        cc_sc[...] = acc

    @pl.when(n == 7)
    def _():
        o_ref[...] = (cc_sc[...] * acc).astype(BF16)

    @pl.when(n >= 8)
    def _():
        o_ref[...] = _sigmoid(acc).astype(BF16)


def _inproj(x2, mod3, g1, lbr, w_in, seq_len, tm):
    t = x2.shape[0]
    tiles_per_seq = seq_len // tm

    def slot(i, n):
        return (jnp.where(n >= 7, n - 1, n), i, 0)

    return pl.pallas_call(
        _inproj_kernel,
        grid=(t // tm, N_SEG),
        in_specs=[pl.BlockSpec((tm, D), lambda i, n: (i, 0)),
                  pl.BlockSpec((1, 6, D), lambda i, n: (i // tiles_per_seq, 0, 0)),
                  pl.BlockSpec((1, D), lambda i, n: (0, 0)),
                  pl.BlockSpec((2, 2, D), lambda i, n: (0, 0, 0)),
                  pl.BlockSpec((D, D), lambda i, n: (0, n))],
        out_specs=pl.BlockSpec((None, tm, D), slot),
        out_shape=jax.ShapeDtypeStruct((9, t, D), BF16),
        scratch_shapes=[pltpu.VMEM((tm, D), BF16), pltpu.VMEM((tm, D), F32)],
        compiler_params=_cparams(("arbitrary", "arbitrary")),
        name="inproj",
    )(x2, mod3, g1, lbr, w_in)


def _hgrn_kernel(*refs, reverse, final, n_chunks):
    if final:
        q_ref, g_ref, v_ref, og_ref, ob_ref, nw_ref, o_ref, st_sc = refs
    else:
        q_ref, g_ref, v_ref, o_ref, st_sc = refs

    @pl.when(pl.program_id(1) == 0)
    def _():
        st_sc[...] = jnp.zeros_like(st_sc)

    def row_of(p):
        return (CH - 1 - p) if reverse else p

    ti = lax.broadcasted_iota(jnp.int32, (CH, CH), 0)
    si = lax.broadcasted_iota(jnp.int32, (CH, CH), 1)
    if reverse:
        pt, ps = CH - 1 - ti, CH - 1 - si
    else:
        pt, ps = ti, si
    causal = ps <= pt
    tri = causal.astype(BF16)
    m1 = ((pt >= 16) & (pt < 32) & (ps < 16)) | ((pt >= 48) & (ps >= 32) & (ps < 48))
    m2 = causal & ((pt // 16) == (ps // 16))

    ri = lax.broadcasted_iota(jnp.int32, (CH, HD), 0)
    pr = (CH - 1 - ri) if reverse else ri
    q0 = pr >= 32
    k0 = pr < 32
    q1a = (pr >= 16) & (pr < 32)
    q1b = pr >= 48
    k1a = pr < 16
    k1b = (pr >= 32) & (pr < 48)
    blk = pr // 16

    def chunk(c, carry):
        cc = (n_chunks - 1 - c) if reverse else c
        rows = pl.ds(pl.multiple_of(cc * CH, CH), CH)
        g_all = g_ref[rows, :]
        b_all = jnp.dot(tri, g_all, preferred_element_type=F32)
        for h in range(HEADS):
            cols = slice(h * HD, (h + 1) * HD)
            q = q_ref[rows, cols].astype(F32)
            v = v_ref[rows, cols]
            g = g_all[:, cols].astype(F32)
            b = b_all[:, cols]
            kk = 1.0 - jnp.exp(g)

            def bro(p):
                return b[row_of(p):row_of(p) + 1, :]

            r0 = bro(31)
            qa = q * jnp.exp(jnp.where(q0, b - r0, NEG))
            ka = kk * jnp.exp(jnp.where(k0, r0 - b, NEG))
            r1a, r1b = bro(15), bro(47)
            qb = q * jnp.exp(jnp.where(q1a, b - r1a, jnp.where(q1b, b - r1b, NEG)))
            kb = kk * jnp.exp(jnp.where(k1a, r1a - b, jnp.where(k1b, r1b - b, NEG)))
            mid = jnp.where(blk == 0, bro(7), jnp.where(blk == 1, bro(23),
                            jnp.where(blk == 2, bro(39), bro(55))))
            qc = q * jnp.exp(jnp.minimum(b - mid, EXP_CLAMP))
            kc = kk * jnp.exp(jnp.minimum(mid - b, EXP_CLAMP))

            def nt(a, bb):
                return lax.dot_general(a.astype(BF16), bb.astype(BF16), (((1,), (1,)), ((), ())),
                                       preferred_element_type=F32)

            att = nt(qa, ka) + jnp.where(m1, nt(qb, kb), 0.0) + jnp.where(m2, nt(qc, kc), 0.0)
            o = jnp.dot(att.astype(BF16), v, preferred_element_type=F32)

            st = st_sc[h]
            o = o + nt(q * jnp.exp(b), st)
            bl = bro(CH - 1)
            kd = (kk * jnp.exp(bl - b)).astype(BF16)
            upd = lax.dot_general(v, kd, (((0,), (0,)), ((), ())), preferred_element_type=F32)
            st_sc[h] = st * jnp.exp(bl) + upd

            if final:
                o = o + ob_ref[rows, cols]
                o = o * lax.rsqrt(jnp.mean(o * o, axis=-1, keepdims=True) + EPS)
                o = o * nw_ref[...] * og_ref[rows, cols].astype(F32)
                o_ref[rows, cols] = o.astype(o_ref.dtype)
            else:
                o_ref[rows, cols] = o
        return carry

    lax.fori_loop(0, n_chunks, chunk, 0)


def _hgrn(proj, o_bw, hg_norm_w, seq_len, ct, reverse, final):
    t = proj.shape[1]
    nt_ = seq_len // ct
    nb = t // seq_len

    def tile(b, j):
        return b * nt_ + ((nt_ - 1 - j) if reverse else j)

    def pspec(slot):
        return pl.BlockSpec((None, ct, D), lambda b, j: (slot, tile(b, j), 0))

    in_specs = [pspec(S_Q), pspec(S_GB if reverse else S_GF), pspec(S_V)]
    args = [proj, proj, proj]
    if final:
        in_specs += [pspec(S_OG), pl.BlockSpec((ct, D), lambda b, j: (tile(b, j), 0)),
                     pl.BlockSpec((1, HD), lambda b, j: (0, 0))]
        args += [proj, o_bw, hg_norm_w]
    return pl.pallas_call(
        functools.partial(_hgrn_kernel, reverse=reverse, final=final, n_chunks=ct // CH),
        grid=(nb, nt_),
        in_specs=in_specs,
        out_specs=pl.BlockSpec((ct, D), lambda b, j: (tile(b, j), 0)),
        out_shape=jax.ShapeDtypeStruct((t, D), BF16 if final else F32),
        scratch_shapes=[pltpu.VMEM((HEADS, HD, HD), F32)],
        compiler_params=_cparams(("arbitrary", "arbitrary")),
        name="hgrn_final" if final else "hgrn_bw",
    )(*args)


def _mix_kernel(og_ref, cb_ref, u_ref, up_ref, un_ref, ga_ref, gb_ref, x_ref, mod_ref, cw_ref,
                wa_ref, wb_ref, wo_ref, g2_ref, s1_ref, s3_ref, s2_ref,
                xsh_ref, h2_ref, *, tiles_per_seq, tm):
    i = pl.program_id(0)
    pos = i % tiles_per_seq
    u = u_ref[...].astype(F32)
    prev_row = jnp.where(pos == 0, 0.0, up_ref[15:16, :].astype(F32))
    next_row = jnp.where(pos == tiles_per_seq - 1, 0.0, un_ref[0:1, :].astype(F32))
    ri = lax.broadcasted_iota(jnp.int32, (tm, D), 0)
    u_prev = jnp.where(ri == 0, prev_row, pltpu.roll(u, 1, axis=0))
    u_next = jnp.where(ri == tm - 1, next_row, pltpu.roll(u, tm - 1, axis=0))
    conv = u_prev * cw_ref[0:1, :] + u * cw_ref[1:2, :] + u_next * cw_ref[2:3, :]
    yb = jnp.dot((cb_ref[...].astype(F32) * conv).astype(BF16), wb_ref[...], preferred_element_type=F32)
    ya = jnp.dot(og_ref[...], wa_ref[...], preferred_element_type=F32)
    merged = ga_ref[...].astype(F32) * ya + gb_ref[...].astype(F32) * yb
    mix = jnp.dot(merged.astype(BF16), wo_ref[...], preferred_element_type=F32)
    gate1 = mod_ref[0, 2:3, :]
    shift2 = mod_ref[0, 3:4, :]
    scale2 = mod_ref[0, 4:5, :]
    gate2 = mod_ref[0, 5:6, :]
    x1 = x_ref[...] + gate1 * mix
    ms = jnp.mean(x1 * x1, axis=-1, keepdims=True)
    h2 = (x1 * lax.rsqrt(ms + EPS) * g2_ref[...]) * (1.0 + scale2) + shift2
    h2_ref[...] = h2
    hb = h2.astype(BF16)
    a = _silu(jnp.dot(hb, s1_ref[...], preferred_element_type=F32)) * jnp.dot(
        hb, s3_ref[...], preferred_element_type=F32)
    shared = jnp.dot(a.astype(BF16), s2_ref[...], preferred_element_type=F32)
    xsh_ref[...] = x1 + gate2 * shared


def _mix(proj, og, x2, mod3, conv_w, w_o_hg, w_o_conv, w_out, g2, s1, s3, s2, seq_len, tm):
    t = x2.shape[0]
    tiles_per_seq = seq_len // tm
    hb = tm // 16
    nhb = t // 16

    def pspec(slot):
        return pl.BlockSpec((None, tm, D), lambda i: (slot, i, 0))

    def full(shape):
        return pl.BlockSpec(shape, lambda i: (0,) * len(shape))

    in_specs = [pl.BlockSpec((tm, D), lambda i: (i, 0)),
                pspec(S_CB), pspec(S_U),
                pl.BlockSpec((None, 16, D), lambda i: (S_U, jnp.maximum(i * hb - 1, 0), 0)),
                pl.BlockSpec((None, 16, D), lambda i: (S_U, jnp.minimum((i + 1) * hb, nhb - 1), 0)),
                pspec(S_GA), pspec(S_GBM),
                pl.BlockSpec((tm, D), lambda i: (i, 0)),
                pl.BlockSpec((1, 6, D), lambda i: (i // tiles_per_seq, 0, 0)),
                full((3, D)), full((D, D)), full((D, D)), full((D, D)), full((1, D)),
                full((D, HID)), full((D, HID)), full((HID, D))]
    return pl.pallas_call(
        functools.partial(_mix_kernel, tiles_per_seq=tiles_per_seq, tm=tm),
        grid=(t // tm,),
        in_specs=in_specs,
        out_specs=[pl.BlockSpec((tm, D), lambda i: (i, 0)), pl.BlockSpec((tm, D), lambda i: (i, 0))],
        out_shape=[jax.ShapeDtypeStruct((t, D), F32), jax.ShapeDtypeStruct((t, D), F32)],
        compiler_params=_cparams(("arbitrary",)),
        name="mix",
    )(og, proj, proj, proj, proj, proj, proj, x2, mod3, conv_w, w_o_hg, w_o_conv, w_out, g2, s1, s3, s2)


def _route_kernel(h_ref, rwt_ref, bias_ref, ustrict_ref, eid_ref, rank_ref, wt_ref, cnt_ref, *, tm):
    @pl.when(pl.program_id(0) == 0)
    def _():
        cnt_ref[...] = jnp.zeros_like(cnt_ref)

    logits = lax.dot_general(rwt_ref[...], h_ref[...], (((1,), (1,)), ((), ())),
                             precision=HIGHEST, preferred_element_type=F32)
    scores = _sigmoid(logits)
    sel = scores + bias_ref[:, 0:1]

    ief = lax.broadcasted_iota(jnp.int32, (N_EXP, tm), 0).astype(F32)
    sel3 = sel.reshape(N_GRP, EXP_PER_GRP, tm)
    i3 = lax.broadcasted_iota(jnp.int32, (N_GRP, EXP_PER_GRP, tm), 1).astype(F32)
    top1 = jnp.max(sel3, axis=1, keepdims=True)
    first = jnp.min(jnp.where(sel3 == top1, i3, float(EXP_PER_GRP)), axis=1, keepdims=True)
    top2 = jnp.max(jnp.where(i3 == first, -jnp.inf, sel3), axis=1, keepdims=True)
    grp = jnp.broadcast_to(top1 + top2, (N_GRP, EXP_PER_GRP, tm)).reshape(N_EXP, tm)
    ig = jnp.floor(ief * (1.0 / EXP_PER_GRP))
    beaten = jnp.zeros((N_EXP, tm), F32)
    for j in range(N_GRP):
        gj = grp[j * EXP_PER_GRP:j * EXP_PER_GRP + 1, :]
        beaten = beaten + jnp.where((gj > grp) | ((gj == grp) & (float(j) < ig)), 1.0, 0.0)
    cand = jnp.where(beaten < float(TOPK_GRP), sel, -jnp.inf)
    chosen = jnp.zeros((N_EXP, tm), F32)
    for _ in range(TOP_K):
        mx = jnp.max(cand, axis=0, keepdims=True)
        idx = jnp.min(jnp.where(cand == mx, ief, float(N_EXP)), axis=0, keepdims=True)
        hit = ief == idx
        chosen = jnp.where(hit, 1.0, chosen)
        cand = jnp.where(hit, -jnp.inf, cand)

    w = scores * chosen
    w = w / jnp.sum(w, axis=0, keepdims=True) * ROUTED_SCALE
    chosen_b = chosen.astype(BF16)
    rank = jnp.dot(chosen_b, ustrict_ref[...], preferred_element_type=F32) + cnt_ref[:, 0:1]
    cnt_ref[...] = cnt_ref[...] + jnp.sum(chosen, axis=1, keepdims=True)
    lstrict = (lax.broadcasted_iota(jnp.int32, (N_EXP, N_EXP), 1)
               < lax.broadcasted_iota(jnp.int32, (N_EXP, N_EXP), 0)).astype(BF16)
    slotpos = jnp.dot(lstrict, chosen_b, preferred_element_type=F32)
    for k in range(TOP_K):
        mk = (chosen > 0.5) & (slotpos == float(k))
        eid_ref[k:k + 1, :] = jnp.sum(jnp.where(mk, ief, 0.0), axis=0, keepdims=True).astype(jnp.int32)
        rank_ref[k:k + 1, :] = jnp.sum(jnp.where(mk, rank, 0.0), axis=0, keepdims=True).astype(jnp.int32)
        wt_ref[k:k + 1, :] = jnp.sum(jnp.where(mk, w, 0.0), axis=0, keepdims=True)


def _route(h2, router_wt, router_bias, tm):
    t = h2.shape[0]
    ustrict = jnp.triu(jnp.ones((tm, tm), BF16), 1)
    bias = jnp.broadcast_to(router_bias.reshape(N_EXP, 1), (N_EXP, LANES))
    kt = lambda i: (0, i)
    return pl.pallas_call(
        functools.partial(_route_kernel, tm=tm),
        grid=(t // tm,),
        in_specs=[pl.BlockSpec((tm, D), lambda i: (i, 0)),
                  pl.BlockSpec((N_EXP, D), lambda i: (0, 0)),
                  pl.BlockSpec((N_EXP, LANES), lambda i: (0, 0)),
                  pl.BlockSpec((tm, tm), lambda i: (0, 0))],
        out_specs=[pl.BlockSpec((TOP_K, tm), kt), pl.BlockSpec((TOP_K, tm), kt),
                   pl.BlockSpec((TOP_K, tm), kt), pl.BlockSpec((N_EXP, LANES), lambda i: (0, 0))],
        out_shape=[jax.ShapeDtypeStruct((TOP_K, t), jnp.int32), jax.ShapeDtypeStruct((TOP_K, t), jnp.int32),
                   jax.ShapeDtypeStruct((TOP_K, t), F32), jax.ShapeDtypeStruct((N_EXP, LANES), F32)],
        compiler_params=_cparams(("arbitrary",)),
        name="route",
    )(h2, router_wt, bias, ustrict)


def _row_tile(ref, row):
    return ref.at[pl.ds(pl.multiple_of(row * SUBLANES, SUBLANES), SUBLANES), :]


def _dispatch_kernel(dest_ref, h_ref, xs_ref, slab, sem, *, tm):
    for j in range(SUBLANES):
        slab[pl.ds(j, tm, stride=SUBLANES), :] = h_ref[:, j * LANES:(j + 1) * LANES]

    def send(t, carry):
        src = _row_tile(slab, t)
        for k in range(TOP_K):
            pltpu.make_async_copy(src, _row_tile(xs_ref, dest_ref[k, t]), sem).start()
        return carry

    lax.fori_loop(0, tm, send, 0)
    for _ in range(TOP_K):
        pltpu.make_async_copy(slab, xs_ref.at[pl.ds(0, tm * SUBLANES), :], sem).wait()


def _dispatch(h2, dest, tm):
    t = h2.shape[0]
    return pl.pallas_call(
        functools.partial(_dispatch_kernel, tm=tm),
        grid=(t // tm,),
        in_specs=[pl.BlockSpec((TOP_K, tm), lambda i: (0, i), memory_space=pltpu.SMEM),
                  pl.BlockSpec((tm, D), lambda i: (i, 0))],
        out_specs=pl.BlockSpec(memory_space=pl.ANY),
        out_shape=jax.ShapeDtypeStruct((t * TOP_K * SUBLANES, LANES), F32),
        scratch_shapes=[pltpu.VMEM((tm * SUBLANES, LANES), F32), pltpu.SemaphoreType.DMA],
        compiler_params=_cparams(("arbitrary",)),
        name="dispatch",
    )(dest, h2)


def _gmm_kernel(blk_ref, eid_ref, lo_ref, hi_ref, first_ref, x_ref, w1_ref, w3_ref, w2_ref, o_ref, *, bm):
    w = pl.program_id(0)
    x = jnp.concatenate([x_ref[pl.ds(j, bm, stride=SUBLANES), :] for j in range(SUBLANES)],
                        axis=1).astype(BF16)
    a = _silu(jnp.dot(x, w1_ref[...], preferred_element_type=F32)) * jnp.dot(
        x, w3_ref[...], preferred_element_type=F32)
    y = jnp.dot(a.astype(BF16), w2_ref[...], preferred_element_type=F32)
    ri = lax.broadcasted_iota(jnp.int32, (bm, LANES), 0)
    mine = (ri >= lo_ref[w]) & (ri < hi_ref[w])

    @pl.when(first_ref[w] == 1)
    def _():
        for j in range(SUBLANES):
            o_ref[pl.ds(j, bm, stride=SUBLANES), :] = jnp.where(mine, y[:, j * LANES:(j + 1) * LANES], 0.0)

    @pl.when(first_ref[w] == 0)
    def _():
        for j in range(SUBLANES):
            rows = pl.ds(j, bm, stride=SUBLANES)
            o_ref[rows, :] = o_ref[rows, :] + jnp.where(mine, y[:, j * LANES:(j + 1) * LANES], 0.0)


def _gmm(xs, meta, w1, w3, w2, bm):
    n_items = meta[0].shape[0]
    grid_spec = pltpu.PrefetchScalarGridSpec(
        num_scalar_prefetch=5,
        grid=(n_items,),
        in_specs=[pl.BlockSpec((bm * SUBLANES, LANES), lambda w, blk, eid, lo, hi, fi: (blk[w], 0)),
                  pl.BlockSpec((None, D, HID), lambda w, blk, eid, lo, hi, fi: (eid[w], 0, 0)),
                  pl.BlockSpec((None, D, HID), lambda w, blk, eid, lo, hi, fi: (eid[w], 0, 0)),
                  pl.BlockSpec((None, HID, D), lambda w, blk, eid, lo, hi, fi: (eid[w], 0, 0))],
        out_specs=pl.BlockSpec((bm * SUBLANES, LANES), lambda w, blk, eid, lo, hi, fi: (blk[w], 0)),
    )
    return pl.pallas_call(
        functools.partial(_gmm_kernel, bm=bm),
        grid_spec=grid_spec,
        out_shape=jax.ShapeDtypeStruct(xs.shape, F32),
        compiler_params=_cparams(("arbitrary",)),
        name="gmm",
    )(*meta, xs, w1, w3, w2)


def _gmm_metadata(counts, n_rows, bm):
    n_blk = n_rows // bm
    n_items = n_blk + N_EXP - 1
    ends = jnp.cumsum(counts)
    starts = ends - counts
    first_blk = starts // bm
    last_blk = jnp.maximum(ends - 1, 0) // bm
    per_e = jnp.where(counts > 0, last_blk - first_blk + 1, 0)
    item_end = jnp.cumsum(per_e)
    item_start = item_end - per_e
    total = item_end[-1]
    w = jnp.arange(n_items, dtype=jnp.int32)
    wc = jnp.minimum(w, total - 1)
    e = jnp.sum((item_end[None, :] <= wc[:, None]).astype(jnp.int32), axis=1)
    blk = (_lookup(first_blk, e) + wc - _lookup(item_start, e)).astype(jnp.int32)
    valid = w < total
    lo = jnp.where(valid, jnp.maximum(_lookup(starts, e), blk * bm) - blk * bm, 0).astype(jnp.int32)
    hi = jnp.where(valid, jnp.minimum(_lookup(ends, e), (blk + 1) * bm) - blk * bm, 0).astype(jnp.int32)
    first = jnp.concatenate([jnp.ones((1,), jnp.int32), (blk[1:] != blk[:-1]).astype(jnp.int32)])
    return blk, e, lo, hi, first, starts


def _lookup(table, idx):
    hit = idx[..., None] == jnp.arange(table.shape[0], dtype=jnp.int32)
    return jnp.sum(jnp.where(hit, table.astype(jnp.int32), 0), axis=-1)


def _combine_kernel(dest_ref, wt_ref, xsh_ref, mod_ref, fg_ref, ys_ref, o_ref, ybuf, sem, *, tm):
    def fetch(t, carry):
        for k in range(TOP_K):
            pltpu.make_async_copy(_row_tile(ys_ref, dest_ref[k, t]), _row_tile(ybuf, k * tm + t), sem).start()
        return carry

    lax.fori_loop(0, tm, fetch, 0)
    pltpu.make_async_copy(ys_ref.at[pl.ds(0, TOP_K * tm * SUBLANES), :], ybuf, sem).wait()

    routed = jnp.zeros((tm, D), F32)
    for k in range(TOP_K):
        yk = jnp.concatenate(
            [ybuf[pl.ds(k * tm * SUBLANES + j, tm, stride=SUBLANES), :] for j in range(SUBLANES)], axis=1)
        routed = routed + wt_ref[:, k:k + 1] * yk
    gate2 = mod_ref[0, 5:6, :]
    x = xsh_ref[...] + gate2 * routed
    ms = jnp.mean(x * x, axis=-1, keepdims=True)
    o_ref[...] = x * lax.rsqrt(ms + EPS) * fg_ref[...]


def _combine(ys, dest, wts_t, xsh, mod3, final_g, seq_len, tm):
    t = xsh.shape[0]
    tiles_per_seq = seq_len // tm
    return pl.pallas_call(
        functools.partial(_combine_kernel, tm=tm),
        grid=(t // tm,),
        in_specs=[pl.BlockSpec((TOP_K, tm), lambda i: (0, i), memory_space=pltpu.SMEM),
                  pl.BlockSpec((tm, TOP_K), lambda i: (i, 0)),
                  pl.BlockSpec((tm, D), lambda i: (i, 0)),
                  pl.BlockSpec((1, 6, D), lambda i: (i // tiles_per_seq, 0, 0)),
                  pl.BlockSpec((1, D), lambda i: (0, 0)),
                  pl.BlockSpec(memory_space=pl.ANY)],
        out_specs=pl.BlockSpec((tm, D), lambda i: (i, 0)),
        out_shape=jax.ShapeDtypeStruct((t, D), F32),
        scratch_shapes=[pltpu.VMEM((TOP_K * tm * SUBLANES, LANES), F32), pltpu.SemaphoreType.DMA],
        compiler_params=_cparams(("arbitrary",)),
        name="combine",
    )(dest, wts_t, xsh, mod3, final_g, ys)


def _pick(n, cap):
    t = cap
    while n % t:
        t //= 2
    return t


def _encode(x, mod, p):
    nb, seq_len, _ = x.shape
    t = nb * seq_len
    x2 = x.reshape(t, D)
    mod3 = mod.reshape(nb, 6, D)

    proj = _inproj(x2, mod3, p["norm1_g"], p["lower_bounds"], p["w_in"], seq_len, _pick(seq_len, 1024))
    ct = _pick(seq_len, 512)
    o_bw = _hgrn(proj, None, None, seq_len, ct, reverse=True, final=False)
    og = _hgrn(proj, o_bw, p["hg_norm_w"], seq_len, ct, reverse=False, final=True)
    xsh, h2 = _mix(proj, og, x2, mod3, p["conv_w"], p["w_o_hg"], p["w_o_conv"], p["w_out"], p["norm2_g"],
                   p["sh_w1"], p["sh_w3"], p["sh_w2"], seq_len, _pick(seq_len, 512))

    eid, rank, wts, cnt = _route(h2, p["router_wt"], p["router_bias"], _pick(t, 1024))
    counts = cnt[:, 0].astype(jnp.int32)
    bm = 256
    blk, item_e, lo, hi, first, starts = _gmm_metadata(counts, t * TOP_K, bm)
    dest = _lookup(starts, eid) + rank
    xs = _dispatch(h2, dest, _pick(t, 512))
    ys = _gmm(xs, (blk, item_e, lo, hi, first), p["exp_w1"], p["exp_w3"], p["exp_w2"], bm)
    out = _combine(ys, dest, wts.T, xsh, mod3, p["final_g"], seq_len, _pick(seq_len, 256))
    return out.reshape(nb, seq_len, D)


def kernel(x_prompt, x_sample, c_prompt, c_sample, w_ada, b_ada, norm1_g, w_in, lower_bounds, hg_norm_w,
           w_o_hg, conv_w, w_o_conv, w_out, norm2_g, router_w, router_bias, exp_w1, exp_w3, exp_w2,
           sh_w1, sh_w3, sh_w2, final_g):
    p = {
        "norm1_g": norm1_g[0].reshape(1, D),
        "lower_bounds": lower_bounds.astype(F32),
        "w_in": w_in[0].astype(BF16),
        "hg_norm_w": hg_norm_w[0].reshape(1, HD),
        "w_o_hg": w_o_hg[0].astype(BF16),
        "conv_w": conv_w[0],
        "w_o_conv": w_o_conv[0].astype(BF16),
        "w_out": w_out[0].astype(BF16),
        "norm2_g": norm2_g[0].reshape(1, D),
        "router_wt": router_w[0].T,
        "router_bias": router_bias[0],
        "exp_w1": exp_w1[0].astype(BF16),
        "exp_w3": exp_w3[0].astype(BF16),
        "exp_w2": exp_w2[0].astype(BF16),
        "sh_w1": sh_w1[0].astype(BF16),
        "sh_w3": sh_w3[0].astype(BF16),
        "sh_w2": sh_w2[0].astype(BF16),
        "final_g": final_g.reshape(1, D),
    }
    nbp = c_prompt.shape[0]
    mod = _ada(jnp.concatenate([c_prompt, c_sample], axis=0), w_ada[0], b_ada[0])
    y_prompt = _encode(x_prompt, mod[:nbp], p)
    y_sample = _encode(x_sample, mod[nbp:], p)
    return (y_prompt, y_sample)
```

```python
import functools

import jax
import jax.numpy as jnp
from jax import lax
from jax.experimental import pallas as pl
from jax.experimental.pallas import tpu as pltpu

F32 = jnp.float32
BF16 = jnp.bfloat16
HIGHEST = lax.Precision.HIGHEST

D = 1024
HEADS = 8
HD = D // HEADS
N_SEG = 10
N_EXP = 64
TOP_K = 8
N_GRP = 8
TOPK_GRP = 4
EXP_PER_GRP = N_EXP // N_GRP
HID = 256
ROUTED_SCALE = 2.5
EPS = 1e-6

CH = 64
UNROLL = 2
NEG = -1e30
EXP_CLAMP = 80.0
LANES = 128
HALF = D // 2
PACK_ROWS = HALF // LANES
GMM_ROWS = 512
GMM_SPLIT = 2
VMEM_LIMIT = 56 * 1024 * 1024

S_Q, S_GF, S_GB, S_V, S_OG, S_CB, S_CC, S_CX, S_GA, S_GBM = range(N_SEG)


def _sigmoid(x):
    return jax.nn.sigmoid(x)


def _silu(x):
    return x * jax.nn.sigmoid(x)


def _cparams(sem):
    return pltpu.CompilerParams(dimension_semantics=sem, vmem_limit_bytes=VMEM_LIMIT)


def _ada_kernel(c_ref, w_ref, b_ref, o_ref):
    s = _silu(c_ref[...])
    o_ref[...] = jnp.dot(s, w_ref[...], precision=HIGHEST, preferred_element_type=F32) + b_ref[...]


def _ada(c, w_ada, b_ada):
    nb = c.shape[0]
    return pl.pallas_call(
        _ada_kernel,
        grid=(6,),
        in_specs=[pl.BlockSpec((nb, D), lambda n: (0, 0)),
                  pl.BlockSpec((D, D), lambda n: (0, n)),
                  pl.BlockSpec((1, D), lambda n: (0, n))],
        out_specs=pl.BlockSpec((nb, D), lambda n: (0, n)),
        out_shape=jax.ShapeDtypeStruct((nb, 6 * D), F32),
        compiler_params=_cparams(("arbitrary",)),
        name="ada",
    )(c, w_ada, b_ada.reshape(1, 6 * D))


def _inproj_kernel(x_ref, mod_ref, g1_ref, lbr_ref, w_ref, o_ref, h_sc):
    n = pl.program_id(1)

    @pl.when(n == 0)
    def _():
        x = x_ref[...]
        ms = jnp.mean(x * x, axis=-1, keepdims=True)
        y = x * lax.rsqrt(ms + EPS) * g1_ref[...]
        shift = mod_ref[0, 0:1, :]
        scale = mod_ref[0, 1:2, :]
        h_sc[...] = (y * (1.0 + scale) + shift).astype(BF16)

    def proj():
        return jnp.dot(h_sc[...], w_ref[...], preferred_element_type=F32)

    @pl.when((n == S_Q) | (n == S_OG))
    def _():
        o_ref[...] = _silu(proj()).astype(BF16)

    @pl.when((n == S_GF) | (n == S_GB))
    def _():
        l0 = lbr_ref[0]
        l1 = lbr_ref[1]
        m = jnp.maximum(l0, l1)
        e0 = jnp.exp(l0 - m)
        lb2 = e0 / (e0 + jnp.exp(l1 - m))
        lb = jnp.where(n == S_GB, lb2[1:2, :], lb2[0:1, :])
        o_ref[...] = jnp.log(lb + (1.0 - lb) * _sigmoid(proj())).astype(BF16)

    @pl.when((n == S_V) | (n == S_CB) | (n == S_CC) | (n == S_CX))
    def _():
        o_ref[...] = proj().astype(BF16)

    @pl.when(n >= S_GA)
    def _():
        o_ref[...] = _sigmoid(proj()).astype(BF16)


def _inproj(x2, mod3, g1, lbr, w_in, seq_len, tm):
    t = x2.shape[0]
    tiles_per_seq = seq_len // tm
    return pl.pallas_call(
        _inproj_kernel,
        grid=(t // tm, N_SEG),
        in_specs=[pl.BlockSpec((tm, D), lambda i, n: (i, 0)),
                  pl.BlockSpec((1, 6, D), lambda i, n: (i // tiles_per_seq, 0, 0)),
                  pl.BlockSpec((1, D), lambda i, n: (0, 0)),
                  pl.BlockSpec((2, 2, D), lambda i, n: (0, 0, 0)),
                  pl.BlockSpec((D, D), lambda i, n: (0, n))],
        out_specs=pl.BlockSpec((None, tm, D), lambda i, n: (n, i, 0)),
        out_shape=jax.ShapeDtypeStruct((N_SEG, t, D), BF16),
        scratch_shapes=[pltpu.VMEM((tm, D), BF16)],
        compiler_params=_cparams(("arbitrary", "arbitrary")),
        name="inproj",
    )(x2, mod3, g1, lbr, w_in)


def _hgrn_kernel(*refs, reverse, final, n_chunks):
    if final:
        q_ref, g_ref, v_ref, og_ref, ob_ref, nw_ref, o_ref, st_sc = refs
    else:
        q_ref, g_ref, v_ref, o_ref, st_sc = refs

    @pl.when(pl.program_id(1) == 0)
    def _():
        st_sc[...] = jnp.zeros_like(st_sc)

    def row_of(p):
        return (CH - 1 - p) if reverse else p

    ti = lax.broadcasted_iota(jnp.int32, (CH, CH), 0)
    si = lax.broadcasted_iota(jnp.int32, (CH, CH), 1)
    if reverse:
        pt, ps = CH - 1 - ti, CH - 1 - si
    else:
        pt, ps = ti, si
    causal = ps <= pt
    tri = causal.astype(BF16)
    m1 = ((pt >= 16) & (pt < 32) & (ps < 16)) | ((pt >= 48) & (ps >= 32) & (ps < 48))
    m2 = causal & ((pt // 16) == (ps // 16))

    ri = lax.broadcasted_iota(jnp.int32, (CH, D), 0)
    pr = (CH - 1 - ri) if reverse else ri
    q0 = pr >= 32
    k0 = pr < 32
    q1a = (pr >= 16) & (pr < 32)
    q1b = pr >= 48
    k1a = pr < 16
    k1b = (pr >= 32) & (pr < 48)
    blk = pr // 16

    def nt(a, bb):
        return lax.dot_general(a, bb, (((1,), (1,)), ((), ())), preferred_element_type=F32)

    def chunk_group(c2, carry):
        for u in range(UNROLL):
            chunk(c2 * UNROLL + u)
        return carry

    def chunk(c):
        cc = (n_chunks - 1 - c) if reverse else c
        rows = pl.ds(pl.multiple_of(cc * CH, CH), CH)
        g16 = g_ref[rows, :]
        b = jnp.dot(tri, g16, preferred_element_type=F32)
        q = q_ref[rows, :].astype(F32)
        kk = 1.0 - jnp.exp(g16.astype(F32))
        v = v_ref[rows, :]

        def bro(p):
            return b[row_of(p):row_of(p) + 1, :]

        r0 = bro(31)
        qa = (q * jnp.exp(jnp.where(q0, b - r0, NEG))).astype(BF16)
        ka = (kk * jnp.exp(jnp.where(k0, r0 - b, NEG))).astype(BF16)
        r1a, r1b = bro(15), bro(47)
        qb = (q * jnp.exp(jnp.where(q1a, b - r1a, jnp.where(q1b, b - r1b, NEG)))).astype(BF16)
        kb = (kk * jnp.exp(jnp.where(k1a, r1a - b, jnp.where(k1b, r1b - b, NEG)))).astype(BF16)
        mid = jnp.where(blk == 0, bro(7), jnp.where(blk == 1, bro(23), jnp.where(blk == 2, bro(39), bro(55))))
        qc = (q * jnp.exp(jnp.minimum(b - mid, EXP_CLAMP))).astype(BF16)
        kc = (kk * jnp.exp(jnp.minimum(mid - b, EXP_CLAMP))).astype(BF16)
        bl = bro(CH - 1)
        qe = (q * jnp.exp(b)).astype(BF16)
        kd = (kk * jnp.exp(bl - b)).astype(BF16)
        dec = jnp.exp(bl)

        for h in range(HEADS):
            cols = slice(h * HD, (h + 1) * HD)
            att = (nt(qa[:, cols], ka[:, cols]) + jnp.where(m1, nt(qb[:, cols], kb[:, cols]), 0.0)
                   + jnp.where(m2, nt(qc[:, cols], kc[:, cols]), 0.0))
            st = st_sc[h]
            o = jnp.dot(att.astype(BF16), v[:, cols], preferred_element_type=F32) + nt(qe[:, cols], st.astype(BF16))
            upd = lax.dot_general(v[:, cols], kd[:, cols], (((0,), (0,)), ((), ())), preferred_element_type=F32)
            st_sc[h] = st * dec[:, cols] + upd
            if final:
                o = o + ob_ref[rows, cols]
                o = o * lax.rsqrt(jnp.mean(o * o, axis=-1, keepdims=True) + EPS)
                o = o * nw_ref[...] * og_ref[rows, cols].astype(F32)
                o_ref[rows, cols] = o.astype(o_ref.dtype)
            else:
                o_ref[rows, cols] = o

    lax.fori_loop(0, n_chunks // UNROLL, chunk_group, 0)


def _hgrn(proj, o_bw, hg_norm_w, seq_len, ct, reverse, final):
    t = proj.shape[1]
    nt_ = seq_len // ct
    nb = t // seq_len

    def tile(b, j):
        return b * nt_ + ((nt_ - 1 - j) if reverse else j)

    def pspec(slot):
        return pl.BlockSpec((None, ct, D), lambda b, j: (slot, tile(b, j), 0))

    in_specs = [pspec(S_Q), pspec(S_GB if reverse else S_GF), pspec(S_V)]
    args = [proj, proj, proj]
    if final:
        in_specs += [pspec(S_OG), pl.BlockSpec((ct, D), lambda b, j: (tile(b, j), 0)),
                     pl.BlockSpec((1, HD), lambda b, j: (0, 0))]
        args += [proj, o_bw, hg_norm_w]
    return pl.pallas_call(
        functools.partial(_hgrn_kernel, reverse=reverse, final=final, n_chunks=ct // CH),
        grid=(nb, nt_),
        in_specs=in_specs,
        out_specs=pl.BlockSpec((ct, D), lambda b, j: (tile(b, j), 0)),
        out_shape=jax.ShapeDtypeStruct((t, D), BF16 if final else F32),
        scratch_shapes=[pltpu.VMEM((HEADS, HD, HD), F32)],
        compiler_params=_cparams(("arbitrary", "arbitrary")),
        name="hgrn_final" if final else "hgrn_bw",
    )(*args)


def _mix_kernel(og_ref, cb_ref, cc_ref, cx_ref, ccp_ref, cxp_ref, ccn_ref, cxn_ref, ga_ref, gb_ref, x_ref,
                mod_ref, cw_ref, wa_ref, wb_ref, wo_ref, g2_ref, s1_ref, s3_ref, s2_ref,
                xsh_ref, h2_ref, *, tiles_per_seq, tm):
    i = pl.program_id(0)
    pos = i % tiles_per_seq
    u = cc_ref[...].astype(F32) * cx_ref[...].astype(F32)
    prev_row = jnp.where(pos == 0, 0.0, ccp_ref[15:16, :].astype(F32) * cxp_ref[15:16, :].astype(F32))
    next_row = jnp.where(pos == tiles_per_seq - 1, 0.0,
                         ccn_ref[0:1, :].astype(F32) * cxn_ref[0:1, :].astype(F32))
    ri = lax.broadcasted_iota(jnp.int32, (tm, D), 0)
    u_prev = jnp.where(ri == 0, prev_row, pltpu.roll(u, 1, axis=0))
    u_next = jnp.where(ri == tm - 1, next_row, pltpu.roll(u, tm - 1, axis=0))
    conv = u_prev * cw_ref[0:1, :] + u * cw_ref[1:2, :] + u_next * cw_ref[2:3, :]
    yb = jnp.dot((cb_ref[...].astype(F32) * conv).astype(BF16), wb_ref[...], preferred_element_type=F32)
    ya = jnp.dot(og_ref[...], wa_ref[...], preferred_element_type=F32)
    merged = ga_ref[...].astype(F32) * ya + gb_ref[...].astype(F32) * yb
    mix = jnp.dot(merged.astype(BF16), wo_ref[...], preferred_element_type=F32)
    gate1 = mod_ref[0, 2:3, :]
    shift2 = mod_ref[0, 3:4, :]
    scale2 = mod_ref[0, 4:5, :]
    gate2 = mod_ref[0, 5:6, :]
    x1 = x_ref[...] + gate1 * mix
    ms = jnp.mean(x1 * x1, axis=-1, keepdims=True)
    h2 = (x1 * lax.rsqrt(ms + EPS) * g2_ref[...]) * (1.0 + scale2) + shift2
    h2_ref[...] = h2
    hb = h2.astype(BF16)
    a = _silu(jnp.dot(hb, s1_ref[...], preferred_element_type=F32)) * jnp.dot(
        hb, s3_ref[...], preferred_element_type=F32)
    shared = jnp.dot(a.astype(BF16), s2_ref[...], preferred_element_type=F32)
    xsh_ref[...] = x1 + gate2 * shared


def _mix(proj, og, x2, mod3, conv_w, w_o_hg, w_o_conv, w_out, g2, s1, s3, s2, seq_len, tm):
    t = x2.shape[0]
    tiles_per_seq = seq_len // tm
    hb = tm // 16
    nhb = t // 16

    def pspec(slot):
        return pl.BlockSpec((None, tm, D), lambda i: (slot, i, 0))

    def halo(slot, nxt):
        if nxt:
            return pl.BlockSpec((None, 16, D), lambda i: (slot, jnp.minimum((i + 1) * hb, nhb - 1), 0))
        return pl.BlockSpec((None, 16, D), lambda i: (slot, jnp.maximum(i * hb - 1, 0), 0))

    def full(shape):
        return pl.BlockSpec(shape, lambda i: (0,) * len(shape))

    in_specs = [pl.BlockSpec((tm, D), lambda i: (i, 0)),
                pspec(S_CB), pspec(S_CC), pspec(S_CX),
                halo(S_CC, False), halo(S_CX, False), halo(S_CC, True), halo(S_CX, True),
                pspec(S_GA), pspec(S_GBM),
                pl.BlockSpec((tm, D), lambda i: (i, 0)),
                pl.BlockSpec((1, 6, D), lambda i: (i // tiles_per_seq, 0, 0)),
                full((3, D)), full((D, D)), full((D, D)), full((D, D)), full((1, D)),
                full((D, HID)), full((D, HID)), full((HID, D))]
    return pl.pallas_call(
        functools.partial(_mix_kernel, tiles_per_seq=tiles_per_seq, tm=tm),
        grid=(t // tm,),
        in_specs=in_specs,
        out_specs=[pl.BlockSpec((tm, D), lambda i: (i, 0)), pl.BlockSpec((tm, D), lambda i: (i, 0))],
        out_shape=[jax.ShapeDtypeStruct((t, D), F32), jax.ShapeDtypeStruct((t, D), F32)],
        compiler_params=_cparams(("arbitrary",)),
        name="mix",
    )(og, proj, proj, proj, proj, proj, proj, proj, proj, proj, x2, mod3, conv_w, w_o_hg, w_o_conv, w_out,
      g2, s1, s3, s2)


def _route_kernel(h_ref, rwt_ref, bias_ref, ustrict_ref, eid_ref, rank_ref, wt_ref, cnt_ref, *, tm):
    @pl.when(pl.program_id(0) == 0)
    def _():
        cnt_ref[...] = jnp.zeros_like(cnt_ref)

    logits = lax.dot_general(rwt_ref[...], h_ref[...], (((1,), (1,)), ((), ())),
                             precision=HIGHEST, preferred_element_type=F32)
    scores = _sigmoid(logits)
    sel = scores + bias_ref[:, 0:1]

    ief = lax.broadcasted_iota(jnp.int32, (N_EXP, tm), 0).astype(F32)
    sel3 = sel.reshape(N_GRP, EXP_PER_GRP, tm)
    i3 = lax.broadcasted_iota(jnp.int32, (N_GRP, EXP_PER_GRP, tm), 1).astype(F32)
    top1 = jnp.max(sel3, axis=1, keepdims=True)
    first = jnp.min(jnp.where(sel3 == top1, i3, float(EXP_PER_GRP)), axis=1, keepdims=True)
    top2 = jnp.max(jnp.where(i3 == first, -jnp.inf, sel3), axis=1, keepdims=True)
    grp = jnp.broadcast_to(top1 + top2, (N_GRP, EXP_PER_GRP, tm)).reshape(N_EXP, tm)
    ig = jnp.floor(ief * (1.0 / EXP_PER_GRP))
    beaten = jnp.zeros((N_EXP, tm), F32)
    for j in range(N_GRP):
        gj = grp[j * EXP_PER_GRP:j * EXP_PER_GRP + 1, :]
        beaten = beaten + jnp.where((gj > grp) | ((gj == grp) & (float(j) < ig)), 1.0, 0.0)
    cand = jnp.where(beaten < float(TOPK_GRP), sel, -jnp.inf)
    chosen = jnp.zeros((N_EXP, tm), F32)
    for _ in range(TOP_K):
        mx = jnp.max(cand, axis=0, keepdims=True)
        idx = jnp.min(jnp.where(cand == mx, ief, float(N_EXP)), axis=0, keepdims=True)
        hit = ief == idx
        chosen = jnp.where(hit, 1.0, chosen)
        cand = jnp.where(hit, -jnp.inf, cand)

    w = scores * chosen
    w = w / jnp.sum(w, axis=0, keepdims=True) * ROUTED_SCALE
    chosen_b = chosen.astype(BF16)
    rank = jnp.dot(chosen_b, ustrict_ref[...], preferred_element_type=F32) + cnt_ref[:, 0:1]
    cnt_ref[...] = cnt_ref[...] + jnp.sum(chosen, axis=1, keepdims=True)
    lstrict = (lax.broadcasted_iota(jnp.int32, (N_EXP, N_EXP), 1)
               < lax.broadcasted_iota(jnp.int32, (N_EXP, N_EXP), 0)).astype(BF16)
    slotpos = jnp.dot(lstrict, chosen_b, preferred_element_type=F32)
    for k in range(TOP_K):
        mk = (chosen > 0.5) & (slotpos == float(k))
        eid_ref[k:k + 1, :] = jnp.sum(jnp.where(mk, ief, 0.0), axis=0, keepdims=True).astype(jnp.int32)
        rank_ref[k:k + 1, :] = jnp.sum(jnp.where(mk, rank, 0.0), axis=0, keepdims=True).astype(jnp.int32)
        wt_ref[k:k + 1, :] = jnp.sum(jnp.where(mk, w, 0.0), axis=0, keepdims=True)


def _route(h2, router_wt, router_bias, tm):
    t = h2.shape[0]
    ustrict = jnp.triu(jnp.ones((tm, tm), BF16), 1)
    bias = jnp.broadcast_to(router_bias.reshape(N_EXP, 1), (N_EXP, LANES))
    kt = lambda i: (0, i)
    return pl.pallas_call(
        functools.partial(_route_kernel, tm=tm),
        grid=(t // tm,),
        in_specs=[pl.BlockSpec((tm, D), lambda i: (i, 0)),
                  pl.BlockSpec((N_EXP, D), lambda i: (0, 0)),
                  pl.BlockSpec((N_EXP, LANES), lambda i: (0, 0)),
                  pl.BlockSpec((tm, tm), lambda i: (0, 0))],
        out_specs=[pl.BlockSpec((TOP_K, tm), kt), pl.BlockSpec((TOP_K, tm), kt),
                   pl.BlockSpec((TOP_K, tm), kt), pl.BlockSpec((N_EXP, LANES), lambda i: (0, 0))],
        out_shape=[jax.ShapeDtypeStruct((TOP_K, t), jnp.int32), jax.ShapeDtypeStruct((TOP_K, t), jnp.int32),
                   jax.ShapeDtypeStruct((TOP_K, t), F32), jax.ShapeDtypeStruct((N_EXP, LANES), F32)],
        compiler_params=_cparams(("arbitrary",)),
        name="route",
    )(h2, router_wt, bias, ustrict)


def _pack_words(x):
    return [pltpu.pack_elementwise([x[:, j * LANES:(j + 1) * LANES], x[:, HALF + j * LANES:HALF + (j + 1) * LANES]],
                                   packed_dtype=BF16) for j in range(PACK_ROWS)]


def _unpack_words(w):
    return tuple(pltpu.unpack_elementwise(w, index=i, packed_dtype=BF16, unpacked_dtype=F32) for i in range(2))


def _token_rows(ref, row):
    return ref.at[pl.ds(pl.multiple_of(row * PACK_ROWS, PACK_ROWS), PACK_ROWS), :]


def _strided(ref, start, m):
    return ref[pl.ds(start, m, stride=PACK_ROWS), :]


def _dispatch_kernel(dest_ref, h_ref, xs_ref, slab, sem, *, tm):
    for j, wj in enumerate(_pack_words(h_ref[...])):
        slab[pl.ds(j, tm, stride=PACK_ROWS), :] = wj

    def send(t, carry):
        src = _token_rows(slab, t)
        for k in range(TOP_K):
            pltpu.make_async_copy(src, _token_rows(xs_ref, dest_ref[k, t]), sem).start(priority=k % 2)
        return carry

    lax.fori_loop(0, tm, send, 0)
    for _ in range(TOP_K):
        pltpu.make_async_copy(slab, xs_ref.at[pl.ds(0, tm * PACK_ROWS), :], sem).wait()


def _dispatch(h2, dest, tm):
    t = h2.shape[0]
    return pl.pallas_call(
        functools.partial(_dispatch_kernel, tm=tm),
        grid=(t // tm,),
        in_specs=[pl.BlockSpec((TOP_K, tm), lambda i: (0, i), memory_space=pltpu.SMEM),
                  pl.BlockSpec((tm, D), lambda i: (i, 0))],
        out_specs=pl.BlockSpec(memory_space=pl.ANY),
        out_shape=jax.ShapeDtypeStruct((t * TOP_K * PACK_ROWS, LANES), jnp.uint32),
        scratch_shapes=[pltpu.VMEM((tm * PACK_ROWS, LANES), jnp.uint32), pltpu.SemaphoreType.DMA],
        compiler_params=_cparams(("arbitrary",)),
        name="dispatch",
    )(dest, h2)


def _gmm_kernel(blk_ref, eid_ref, x_ref, w1_ref, w3_ref, w2_ref, o_ref, *, bm):
    sub = bm // GMM_SPLIT
    for s in range(GMM_SPLIT):
        base = s * sub * PACK_ROWS
        parts = [_unpack_words(_strided(x_ref, base + j, sub)) for j in range(PACK_ROWS)]
        x = jnp.concatenate([p[0] for p in parts] + [p[1] for p in parts], axis=1).astype(BF16)
        a = _silu(jnp.dot(x, w1_ref[...], preferred_element_type=F32)) * jnp.dot(
            x, w3_ref[...], preferred_element_type=F32)
        y = jnp.dot(a.astype(BF16), w2_ref[...], preferred_element_type=F32)
        for j, wj in enumerate(_pack_words(y)):
            o_ref[pl.ds(base + j, sub, stride=PACK_ROWS), :] = wj


def _gmm(xs, blk, item_e, w1, w3, w2, bm):
    n_items = blk.shape[0]
    grid_spec = pltpu.PrefetchScalarGridSpec(
        num_scalar_prefetch=2,
        grid=(n_items,),
        in_specs=[pl.BlockSpec((bm * PACK_ROWS, LANES), lambda w, blk, eid: (blk[w], 0)),
                  pl.BlockSpec((None, D, HID), lambda w, blk, eid: (eid[w], 0, 0)),
                  pl.BlockSpec((None, D, HID), lambda w, blk, eid: (eid[w], 0, 0)),
                  pl.BlockSpec((None, HID, D), lambda w, blk, eid: (eid[w], 0, 0))],
        out_specs=pl.BlockSpec((bm * PACK_ROWS, LANES), lambda w, blk, eid: (w, 0)),
    )
    return pl.pallas_call(
        functools.partial(_gmm_kernel, bm=bm),
        grid_spec=grid_spec,
        out_shape=jax.ShapeDtypeStruct((n_items * bm * PACK_ROWS, LANES), xs.dtype),
        compiler_params=_cparams(("arbitrary",)),
        name="gmm",
    )(blk, item_e, xs, w1, w3, w2)


def _gmm_metadata(counts, n_rows, bm):
    n_blk = n_rows // bm
    n_items = n_blk + N_EXP - 1
    ends = jnp.cumsum(counts)
    starts = ends - counts
    first_blk = starts // bm
    last_blk = jnp.maximum(ends - 1, 0) // bm
    per_e = jnp.where(counts > 0, last_blk - first_blk + 1, 0)
    item_end = jnp.cumsum(per_e)
    item_start = item_end - per_e
    total = item_end[-1]
    w = jnp.minimum(jnp.arange(n_items, dtype=jnp.int32), total - 1)
    e = jnp.sum((item_end[None, :] <= w[:, None]).astype(jnp.int32), axis=1)
    blk = (_lookup(first_blk, e) + w - _lookup(item_start, e)).astype(jnp.int32)
    return blk, e, starts, item_start - first_blk


def _lookup(table, idx):
    hit = idx[..., None] == jnp.arange(table.shape[0], dtype=jnp.int32)
    return jnp.sum(jnp.where(hit, table.astype(jnp.int32), 0), axis=-1)


def _combine_kernel(dest_ref, wt_ref, xsh_ref, mod_ref, fg_ref, ys_ref, o_ref, ybuf, sem, *, tm):
    def fetch(t, carry):
        for k in range(TOP_K):
            pltpu.make_async_copy(_token_rows(ys_ref, dest_ref[k, t]), _token_rows(ybuf, k * tm + t),
                                  sem).start(priority=k % 2)
        return carry

    lax.fori_loop(0, tm, fetch, 0)
    pltpu.make_async_copy(ys_ref.at[pl.ds(0, TOP_K * tm * PACK_ROWS), :], ybuf, sem).wait()

    lo = [jnp.zeros((tm, LANES), F32) for _ in range(PACK_ROWS)]
    hi = [jnp.zeros((tm, LANES), F32) for _ in range(PACK_ROWS)]
    for k in range(TOP_K):
        wk = wt_ref[:, k:k + 1]
        for j in range(PACK_ROWS):
            l, h = _unpack_words(_strided(ybuf, k * tm * PACK_ROWS + j, tm))
            lo[j] = lo[j] + wk * l
            hi[j] = hi[j] + wk * h
    routed = jnp.concatenate(lo + hi, axis=1)
    gate2 = mod_ref[0, 5:6, :]
    x = xsh_ref[...] + gate2 * routed
    ms = jnp.mean(x * x, axis=-1, keepdims=True)
    o_ref[...] = x * lax.rsqrt(ms + EPS) * fg_ref[...]


def _combine(ys, dest, wts_t, xsh, mod3, final_g, seq_len, tm):
    t = xsh.shape[0]
    tiles_per_seq = seq_len // tm
    return pl.pallas_call(
        functools.partial(_combine_kernel, tm=tm),
        grid=(t // tm,),
        in_specs=[pl.BlockSpec((TOP_K, tm), lambda i: (0, i), memory_space=pltpu.SMEM),
                  pl.BlockSpec((tm, TOP_K), lambda i: (i, 0)),
                  pl.BlockSpec((tm, D), lambda i: (i, 0)),
                  pl.BlockSpec((1, 6, D), lambda i: (i // tiles_per_seq, 0, 0)),
                  pl.BlockSpec((1, D), lambda i: (0, 0)),
                  pl.BlockSpec(memory_space=pl.ANY)],
        out_specs=pl.BlockSpec((tm, D), lambda i: (i, 0)),
        out_shape=jax.ShapeDtypeStruct((t, D), F32),
        scratch_shapes=[pltpu.VMEM((TOP_K * tm * PACK_ROWS, LANES), jnp.uint32), pltpu.SemaphoreType.DMA],
        compiler_params=_cparams(("arbitrary",)),
        name="combine",
    )(dest, wts_t, xsh, mod3, final_g, ys)


def _pick(n, cap):
    t = cap
    while n % t:
        t //= 2
    return t


def _encode(x, mod, p):
    nb, seq_len, _ = x.shape
    t = nb * seq_len
    x2 = x.reshape(t, D)
    mod3 = mod.reshape(nb, 6, D)

    proj = _inproj(x2, mod3, p["norm1_g"], p["lower_bounds"], p["w_in"], seq_len, _pick(seq_len, 1024))
    ct = _pick(seq_len, 512)
    o_bw = _hgrn(proj, None, None, seq_len, ct, reverse=True, final=False)
    og = _hgrn(proj, o_bw, p["hg_norm_w"], seq_len, ct, reverse=False, final=True)
    xsh, h2 = _mix(proj, og, x2, mod3, p["conv_w"], p["w_o_hg"], p["w_o_conv"], p["w_out"], p["norm2_g"],
                   p["sh_w1"], p["sh_w3"], p["sh_w2"], seq_len, _pick(seq_len, 512))

    eid, rank, wts, cnt = _route(h2, p["router_wt"], p["router_bias"], _pick(t, 1024))
    counts = cnt[:, 0].astype(jnp.int32)
    bm = _pick(t * TOP_K, GMM_ROWS)
    blk, item_e, starts, item_shift = _gmm_metadata(counts, t * TOP_K, bm)
    dest = _lookup(starts, eid) + rank
    xs = _dispatch(h2, dest, _pick(t, 512))
    ys = _gmm(xs, blk, item_e, p["exp_w1"], p["exp_w3"], p["exp_w2"], bm)
    src = dest + _lookup(item_shift, eid) * bm
    out = _combine(ys, src, wts.T, xsh, mod3, p["final_g"], seq_len, _pick(seq_len, 256))
    return out.reshape(nb, seq_len, D)


def kernel(x_prompt, x_sample, c_prompt, c_sample, w_ada, b_ada, norm1_g, w_in, lower_bounds, hg_norm_w,
           w_o_hg, conv_w, w_o_conv, w_out, norm2_g, router_w, router_bias, exp_w1, exp_w3, exp_w2,
           sh_w1, sh_w3, sh_w2, final_g):
    p = {
        "norm1_g": norm1_g[0].reshape(1, D),
        "lower_bounds": lower_bounds.astype(F32),
        "w_in": w_in[0].astype(BF16),
        "hg_norm_w": hg_norm_w[0].reshape(1, HD),
        "w_o_hg": w_o_hg[0].astype(BF16),
        "conv_w": conv_w[0],
        "w_o_conv": w_o_conv[0].astype(BF16),
        "w_out": w_out[0].astype(BF16),
        "norm2_g": norm2_g[0].reshape(1, D),
        "router_wt": router_w[0].T,
        "router_bias": router_bias[0],
        "exp_w1": exp_w1[0].astype(BF16),
        "exp_w3": exp_w3[0].astype(BF16),
        "exp_w2": exp_w2[0].astype(BF16),
        "sh_w1": sh_w1[0].astype(BF16),
        "sh_w3": sh_w3[0].astype(BF16),
        "sh_w2": sh_w2[0].astype(BF16),
        "final_g": final_g.reshape(1, D),
    }
    nbp = c_prompt.shape[0]
    mod = _ada(jnp.concatenate([c_prompt, c_sample], axis=0), w_ada[0], b_ada[0])
    y_prompt = _encode(x_prompt, mod[:nbp], p)
    y_sample = _encode(x_sample, mod[nbp:], p)
    return (y_prompt, y_sample)
```

```python
import functools

import jax
import jax.numpy as jnp
from jax import lax
from jax.experimental import pallas as pl
from jax.experimental.pallas import tpu as pltpu

F32 = jnp.float32
BF16 = jnp.bfloat16
HIGHEST = lax.Precision.HIGHEST

D = 1024
HEADS = 8
HD = D // HEADS
N_SEG = 10
N_EXP = 64
TOP_K = 8
N_GRP = 8
TOPK_GRP = 4
EXP_PER_GRP = N_EXP // N_GRP
HID = 256
ROUTED_SCALE = 2.5
EPS = 1e-6

CH = 64
UNROLL = 2
EXP2_CLAMP = 115.0
LANES = 128
HALF = D // 2
PACK_ROWS = HALF // LANES
GMM_ROWS = 512
GMM_SPLIT = 2
VMEM_LIMIT = 56 * 1024 * 1024

S_Q, S_GF, S_GB, S_V, S_OG, S_CB, S_CC, S_CX, S_GA, S_GBM = range(N_SEG)


def _sigmoid(x):
    return jax.nn.sigmoid(x)


def _silu(x):
    return x * jax.nn.sigmoid(x)


def _cparams(sem):
    return pltpu.CompilerParams(dimension_semantics=sem, vmem_limit_bytes=VMEM_LIMIT)


def _ada_kernel(c_ref, w_ref, b_ref, o_ref):
    s = _silu(c_ref[...])
    o_ref[...] = jnp.dot(s, w_ref[...], precision=HIGHEST, preferred_element_type=F32) + b_ref[...]


def _ada(c, w_ada, b_ada):
    nb = c.shape[0]
    return pl.pallas_call(
        _ada_kernel,
        grid=(6,),
        in_specs=[pl.BlockSpec((nb, D), lambda n: (0, 0)),
                  pl.BlockSpec((D, D), lambda n: (0, n)),
                  pl.BlockSpec((1, D), lambda n: (0, n))],
        out_specs=pl.BlockSpec((nb, D), lambda n: (0, n)),
        out_shape=jax.ShapeDtypeStruct((nb, 6 * D), F32),
        compiler_params=_cparams(("arbitrary",)),
        name="ada",
    )(c, w_ada, b_ada.reshape(1, 6 * D))


def _inproj_kernel(x_ref, mod_ref, g1_ref, lbr_ref, w_ref, o_ref, h_sc):
    n = pl.program_id(1)

    @pl.when(n == 0)
    def _():
        x = x_ref[...]
        ms = jnp.mean(x * x, axis=-1, keepdims=True)
        y = x * lax.rsqrt(ms + EPS) * g1_ref[...]
        shift = mod_ref[0, 0:1, :]
        scale = mod_ref[0, 1:2, :]
        h_sc[...] = (y * (1.0 + scale) + shift).astype(BF16)

    def proj():
        return jnp.dot(h_sc[...], w_ref[...], preferred_element_type=F32)

    @pl.when((n == S_Q) | (n == S_OG))
    def _():
        o_ref[...] = _silu(proj()).astype(BF16)

    @pl.when((n == S_GF) | (n == S_GB))
    def _():
        l0 = lbr_ref[0]
        l1 = lbr_ref[1]
        m = jnp.maximum(l0, l1)
        e0 = jnp.exp(l0 - m)
        lb2 = e0 / (e0 + jnp.exp(l1 - m))
        lb = jnp.where(n == S_GB, lb2[1:2, :], lb2[0:1, :])
        o_ref[...] = jnp.log2(lb + (1.0 - lb) * _sigmoid(proj())).astype(BF16)

    @pl.when((n == S_V) | (n == S_CB) | (n == S_CC) | (n == S_CX))
    def _():
        o_ref[...] = proj().astype(BF16)

    @pl.when(n >= S_GA)
    def _():
        o_ref[...] = _sigmoid(proj()).astype(BF16)


def _inproj(x2, mod3, g1, lbr, w_in, seq_len, tm):
    t = x2.shape[0]
    tiles_per_seq = seq_len // tm
    return pl.pallas_call(
        _inproj_kernel,
        grid=(t // tm, N_SEG),
        in_specs=[pl.BlockSpec((tm, D), lambda i, n: (i, 0)),
                  pl.BlockSpec((1, 6, D), lambda i, n: (i // tiles_per_seq, 0, 0)),
                  pl.BlockSpec((1, D), lambda i, n: (0, 0)),
                  pl.BlockSpec((2, 2, D), lambda i, n: (0, 0, 0)),
                  pl.BlockSpec((D, D), lambda i, n: (0, n))],
        out_specs=pl.BlockSpec((None, tm, D), lambda i, n: (n, i, 0)),
        out_shape=jax.ShapeDtypeStruct((N_SEG, t, D), BF16),
        scratch_shapes=[pltpu.VMEM((tm, D), BF16)],
        compiler_params=_cparams(("arbitrary", "arbitrary")),
        name="inproj",
    )(x2, mod3, g1, lbr, w_in)


def _hgrn_kernel(*refs, reverse, final, n_chunks):
    if final:
        q_ref, g_ref, v_ref, og_ref, ob_ref, nw_ref, o_ref, st_sc = refs
    else:
        q_ref, g_ref, v_ref, o_ref, st_sc = refs

    @pl.when(pl.program_id(1) == 0)
    def _():
        st_sc[...] = jnp.zeros_like(st_sc)

    def row_of(p):
        return (CH - 1 - p) if reverse else p

    ti = lax.broadcasted_iota(jnp.int32, (CH, CH), 0)
    si = lax.broadcasted_iota(jnp.int32, (CH, CH), 1)
    if reverse:
        pt, ps = CH - 1 - ti, CH - 1 - si
    else:
        pt, ps = ti, si
    causal = ps <= pt
    tri = causal.astype(BF16)
    m1 = ((pt >= 16) & (pt < 32) & (ps < 16)) | ((pt >= 48) & (ps >= 32) & (ps < 48))
    m2 = causal & ((pt // 16) == (ps // 16))

    ri = lax.broadcasted_iota(jnp.int32, (CH, D), 0)
    pr = (CH - 1 - ri) if reverse else ri
    blk = pr // 16

    def nt(a, bb):
        return lax.dot_general(a, bb, (((1,), (1,)), ((), ())), preferred_element_type=F32)

    def seg(x, p0, p1):
        return x[CH - p1:CH - p0] if reverse else x[p0:p1]

    def place(pieces):
        pieces = sorted(pieces, key=lambda t: -t[0] if reverse else t[0])
        out, pos = [], 0
        for p0, p1, x in pieces:
            lo = (CH - p1) if reverse else p0
            if lo > pos:
                out.append(jnp.zeros((lo - pos, D), BF16))
            out.append(x)
            pos = lo + (p1 - p0)
        if pos < CH:
            out.append(jnp.zeros((CH - pos, D), BF16))
        return jnp.concatenate(out, axis=0)

    def chunk_group(c2, carry):
        for u in range(UNROLL):
            chunk(c2 * UNROLL + u)
        return carry

    def chunk(c):
        cc = (n_chunks - 1 - c) if reverse else c
        rows = pl.ds(pl.multiple_of(cc * CH, CH), CH)
        g16 = g_ref[rows, :]
        b = jnp.dot(tri, g16, preferred_element_type=F32)
        q = q_ref[rows, :].astype(F32)
        kk = 1.0 - jnp.exp2(g16.astype(F32))
        v = v_ref[rows, :]

        def bro(p):
            return b[row_of(p):row_of(p) + 1, :]

        def qside(p0, p1, ref):
            return (p0, p1, (seg(q, p0, p1) * jnp.exp2(seg(b, p0, p1) - ref)).astype(BF16))

        def kside(p0, p1, ref):
            return (p0, p1, (seg(kk, p0, p1) * jnp.exp2(ref - seg(b, p0, p1))).astype(BF16))

        r0 = bro(31)
        qa = place([qside(32, 64, r0)])
        ka = place([kside(0, 32, r0)])
        r1a, r1b = bro(15), bro(47)
        qb = place([qside(16, 32, r1a), qside(48, 64, r1b)])
        kb = place([kside(0, 16, r1a), kside(32, 48, r1b)])
        mid = jnp.where(blk == 0, bro(7), jnp.where(blk == 1, bro(23), jnp.where(blk == 2, bro(39), bro(55))))
        qc = (q * jnp.exp2(jnp.minimum(b - mid, EXP2_CLAMP))).astype(BF16)
        kc = (kk * jnp.exp2(jnp.minimum(mid - b, EXP2_CLAMP))).astype(BF16)
        bl = bro(CH - 1)
        qe = (q * jnp.exp2(b)).astype(BF16)
        kd = (kk * jnp.exp2(bl - b)).astype(BF16)
        dec = jnp.exp2(bl)

        heads = [slice(h * HD, (h + 1) * HD) for h in range(HEADS)]
        att = [(nt(qa[:, c], ka[:, c]) + jnp.where(m1, nt(qb[:, c], kb[:, c]), 0.0)
                + jnp.where(m2, nt(qc[:, c], kc[:, c]), 0.0)).astype(BF16) for c in heads]
        st = [st_sc[h] for h in range(HEADS)]
        inter = [nt(qe[:, c], st[h].astype(BF16)) for h, c in enumerate(heads)]
        upd = [lax.dot_general(v[:, c], kd[:, c], (((0,), (0,)), ((), ())), preferred_element_type=F32)
               for c in heads]
        intra = [jnp.dot(att[h], v[:, c], preferred_element_type=F32) for h, c in enumerate(heads)]
        for h, c in enumerate(heads):
            st_sc[h] = st[h] * dec[:, c] + upd[h]
            o = intra[h] + inter[h]
            if final:
                o = o + ob_ref[rows, c]
                o = o * lax.rsqrt(jnp.mean(o * o, axis=-1, keepdims=True) + EPS)
                o = o * nw_ref[...] * og_ref[rows, c].astype(F32)
                o_ref[rows, c] = o.astype(o_ref.dtype)
            else:
                o_ref[rows, c] = o

    lax.fori_loop(0, n_chunks // UNROLL, chunk_group, 0)


def _hgrn(proj, o_bw, hg_norm_w, seq_len, ct, reverse, final):
    t = proj.shape[1]
    nt_ = seq_len // ct
    nb = t // seq_len

    def tile(b, j):
        return b * nt_ + ((nt_ - 1 - j) if reverse else j)

    def pspec(slot):
        return pl.BlockSpec((None, ct, D), lambda b, j: (slot, tile(b, j), 0))

    in_specs = [pspec(S_Q), pspec(S_GB if reverse else S_GF), pspec(S_V)]
    args = [proj, proj, proj]
    if final:
        in_specs += [pspec(S_OG), pl.BlockSpec((ct, D), lambda b, j: (tile(b, j), 0)),
                     pl.BlockSpec((1, HD), lambda b, j: (0, 0))]
        args += [proj, o_bw, hg_norm_w]
    return pl.pallas_call(
        functools.partial(_hgrn_kernel, reverse=reverse, final=final, n_chunks=ct // CH),
        grid=(nb, nt_),
        in_specs=in_specs,
        out_specs=pl.BlockSpec((ct, D), lambda b, j: (tile(b, j), 0)),
        out_shape=jax.ShapeDtypeStruct((t, D), BF16 if final else F32),
        scratch_shapes=[pltpu.VMEM((HEADS, HD, HD), F32)],
        compiler_params=_cparams(("arbitrary", "arbitrary")),
        name="hgrn_final" if final else "hgrn_bw",
    )(*args)


def _mix_kernel(og_ref, cb_ref, cc_ref, cx_ref, ccp_ref, cxp_ref, ccn_ref, cxn_ref, ga_ref, gb_ref, x_ref,
                mod_ref, cw_ref, wa_ref, wb_ref, wo_ref, g2_ref, s1_ref, s3_ref, s2_ref,
                xsh_ref, h2_ref, *, tiles_per_seq, tm):
    i = pl.program_id(0)
    pos = i % tiles_per_seq
    u = cc_ref[...].astype(F32) * cx_ref[...].astype(F32)
    prev_row = jnp.where(pos == 0, 0.0, ccp_ref[15:16, :].astype(F32) * cxp_ref[15:16, :].astype(F32))
    next_row = jnp.where(pos == tiles_per_seq - 1, 0.0,
                         ccn_ref[0:1, :].astype(F32) * cxn_ref[0:1, :].astype(F32))
    ri = lax.broadcasted_iota(jnp.int32, (tm, D), 0)
    u_prev = jnp.where(ri == 0, prev_row, pltpu.roll(u, 1, axis=0))
    u_next = jnp.where(ri == tm - 1, next_row, pltpu.roll(u, tm - 1, axis=0))
    conv = u_prev * cw_ref[0:1, :] + u * cw_ref[1:2, :] + u_next * cw_ref[2:3, :]
    yb = jnp.dot((cb_ref[...].astype(F32) * conv).astype(BF16), wb_ref[...], preferred_element_type=F32)
    ya = jnp.dot(og_ref[...], wa_ref[...], preferred_element_type=F32)
    merged = ga_ref[...].astype(F32) * ya + gb_ref[...].astype(F32) * yb
    mix = jnp.dot(merged.astype(BF16), wo_ref[...], preferred_element_type=F32)
    gate1 = mod_ref[0, 2:3, :]
    shift2 = mod_ref[0, 3:4, :]
    scale2 = mod_ref[0, 4:5, :]
    gate2 = mod_ref[0, 5:6, :]
    x1 = x_ref[...] + gate1 * mix
    ms = jnp.mean(x1 * x1, axis=-1, keepdims=True)
    h2 = (x1 * lax.rsqrt(ms + EPS) * g2_ref[...]) * (1.0 + scale2) + shift2
    h2_ref[...] = h2
    hb = h2.astype(BF16)
    a = _silu(jnp.dot(hb, s1_ref[...], preferred_element_type=F32)) * jnp.dot(
        hb, s3_ref[...], preferred_element_type=F32)
    shared = jnp.dot(a.astype(BF16), s2_ref[...], preferred_element_type=F32)
    xsh_ref[...] = x1 + gate2 * shared


def _mix(proj, og, x2, mod3, conv_w, w_o_hg, w_o_conv, w_out, g2, s1, s3, s2, seq_len, tm):
    t = x2.shape[0]
    tiles_per_seq = seq_len // tm
    hb = tm // 16
    nhb = t // 16

    def pspec(slot):
        return pl.BlockSpec((None, tm, D), lambda i: (slot, i, 0))

    def halo(slot, nxt):
        if nxt:
            return pl.BlockSpec((None, 16, D), lambda i: (slot, jnp.minimum((i + 1) * hb, nhb - 1), 0))
        return pl.BlockSpec((None, 16, D), lambda i: (slot, jnp.maximum(i * hb - 1, 0), 0))

    def full(shape):
        return pl.BlockSpec(shape, lambda i: (0,) * len(shape))

    in_specs = [pl.BlockSpec((tm, D), lambda i: (i, 0)),
                pspec(S_CB), pspec(S_CC), pspec(S_CX),
                halo(S_CC, False), halo(S_CX, False), halo(S_CC, True), halo(S_CX, True),
                pspec(S_GA), pspec(S_GBM),
                pl.BlockSpec((tm, D), lambda i: (i, 0)),
                pl.BlockSpec((1, 6, D), lambda i: (i // tiles_per_seq, 0, 0)),
                full((3, D)), full((D, D)), full((D, D)), full((D, D)), full((1, D)),
                full((D, HID)), full((D, HID)), full((HID, D))]
    return pl.pallas_call(
        functools.partial(_mix_kernel, tiles_per_seq=tiles_per_seq, tm=tm),
        grid=(t // tm,),
        in_specs=in_specs,
        out_specs=[pl.BlockSpec((tm, D), lambda i: (i, 0)), pl.BlockSpec((tm, D), lambda i: (i, 0))],
        out_shape=[jax.ShapeDtypeStruct((t, D), F32), jax.ShapeDtypeStruct((t, D), F32)],
        compiler_params=_cparams(("arbitrary",)),
        name="mix",
    )(og, proj, proj, proj, proj, proj, proj, proj, proj, proj, x2, mod3, conv_w, w_o_hg, w_o_conv, w_out,
      g2, s1, s3, s2)


def _route_kernel(h_ref, rwt_ref, bias_ref, ustrict_ref, eid_ref, rank_ref, wt_ref, cnt_ref, *, tm):
    @pl.when(pl.program_id(0) == 0)
    def _():
        cnt_ref[...] = jnp.zeros_like(cnt_ref)

    logits = lax.dot_general(rwt_ref[...], h_ref[...], (((1,), (1,)), ((), ())),
                             precision=HIGHEST, preferred_element_type=F32)
    scores = _sigmoid(logits)
    sel = scores + bias_ref[:, 0:1]

    ief = lax.broadcasted_iota(jnp.int32, (N_EXP, tm), 0).astype(F32)
    sel3 = sel.reshape(N_GRP, EXP_PER_GRP, tm)
    i3 = lax.broadcasted_iota(jnp.int32, (N_GRP, EXP_PER_GRP, tm), 1).astype(F32)
    top1 = jnp.max(sel3, axis=1, keepdims=True)
    first = jnp.min(jnp.where(sel3 == top1, i3, float(EXP_PER_GRP)), axis=1, keepdims=True)
    top2 = jnp.max(jnp.where(i3 == first, -jnp.inf, sel3), axis=1, keepdims=True)
    grp = jnp.broadcast_to(top1 + top2, (N_GRP, EXP_PER_GRP, tm)).reshape(N_EXP, tm)
    ig = jnp.floor(ief * (1.0 / EXP_PER_GRP))
    beaten = jnp.zeros((N_EXP, tm), F32)
    for j in range(N_GRP):
        gj = grp[j * EXP_PER_GRP:j * EXP_PER_GRP + 1, :]
        beaten = beaten + jnp.where((gj > grp) | ((gj == grp) & (float(j) < ig)), 1.0, 0.0)
    cand = jnp.where(beaten < float(TOPK_GRP), sel, -jnp.inf)
    chosen = jnp.zeros((N_EXP, tm), F32)
    for _ in range(TOP_K):
        mx = jnp.max(cand, axis=0, keepdims=True)
        idx = jnp.min(jnp.where(cand == mx, ief, float(N_EXP)), axis=0, keepdims=True)
        hit = ief == idx
        chosen = jnp.where(hit, 1.0, chosen)
        cand = jnp.where(hit, -jnp.inf, cand)

    w = scores * chosen
    w = w / jnp.sum(w, axis=0, keepdims=True) * ROUTED_SCALE
    chosen_b = chosen.astype(BF16)
    rank = jnp.dot(chosen_b, ustrict_ref[...], preferred_element_type=F32) + cnt_ref[:, 0:1]
    cnt_ref[...] = cnt_ref[...] + jnp.sum(chosen, axis=1, keepdims=True)
    lstrict = (lax.broadcasted_iota(jnp.int32, (N_EXP, N_EXP), 1)
               < lax.broadcasted_iota(jnp.int32, (N_EXP, N_EXP), 0)).astype(BF16)
    slotpos = jnp.dot(lstrict, chosen_b, preferred_element_type=F32)
    for k in range(TOP_K):
        mk = (chosen > 0.5) & (slotpos == float(k))
        eid_ref[k:k + 1, :] = jnp.sum(jnp.where(mk, ief, 0.0), axis=0, keepdims=True).astype(jnp.int32)
        rank_ref[k:k + 1, :] = jnp.sum(jnp.where(mk, rank, 0.0), axis=0, keepdims=True).astype(jnp.int32)
        wt_ref[k:k + 1, :] = jnp.sum(jnp.where(mk, w, 0.0), axis=0, keepdims=True)


def _route(h2, router_wt, router_bias, tm):
    t = h2.shape[0]
    ustrict = jnp.triu(jnp.ones((tm, tm), BF16), 1)
    bias = jnp.broadcast_to(router_bias.reshape(N_EXP, 1), (N_EXP, LANES))
    kt = lambda i: (0, i)
    return pl.pallas_call(
        functools.partial(_route_kernel, tm=tm),
        grid=(t // tm,),
        in_specs=[pl.BlockSpec((tm, D), lambda i: (i, 0)),
                  pl.BlockSpec((N_EXP, D), lambda i: (0, 0)),
                  pl.BlockSpec((N_EXP, LANES), lambda i: (0, 0)),
                  pl.BlockSpec((tm, tm), lambda i: (0, 0))],
        out_specs=[pl.BlockSpec((TOP_K, tm), kt), pl.BlockSpec((TOP_K, tm), kt),
                   pl.BlockSpec((TOP_K, tm), kt), pl.BlockSpec((N_EXP, LANES), lambda i: (0, 0))],
        out_shape=[jax.ShapeDtypeStruct((TOP_K, t), jnp.int32), jax.ShapeDtypeStruct((TOP_K, t), jnp.int32),
                   jax.ShapeDtypeStruct((TOP_K, t), F32), jax.ShapeDtypeStruct((N_EXP, LANES), F32)],
        compiler_params=_cparams(("arbitrary",)),
        name="route",
    )(h2, router_wt, bias, ustrict)


def _pack_words(x):
    return [pltpu.pack_elementwise([x[:, j * LANES:(j + 1) * LANES], x[:, HALF + j * LANES:HALF + (j + 1) * LANES]],
                                   packed_dtype=BF16) for j in range(PACK_ROWS)]


def _unpack_words(w):
    return tuple(pltpu.unpack_elementwise(w, index=i, packed_dtype=BF16, unpacked_dtype=F32) for i in range(2))


def _token_rows(ref, row):
    return ref.at[pl.ds(pl.multiple_of(row * PACK_ROWS, PACK_ROWS), PACK_ROWS), :]


def _strided(ref, start, m):
    return ref[pl.ds(start, m, stride=PACK_ROWS), :]


def _dispatch_kernel(dest_ref, h_ref, xs_ref, slab, sem, *, tm):
    for j, wj in enumerate(_pack_words(h_ref[...])):
        slab[pl.ds(j, tm, stride=PACK_ROWS), :] = wj

    def send(t, carry):
        src = _token_rows(slab, t)
        for k in range(TOP_K):
            pltpu.make_async_copy(src, _token_rows(xs_ref, dest_ref[k, t]), sem).start(priority=k % 2)
        return carry

    lax.fori_loop(0, tm, send, 0)
    for _ in range(TOP_K):
        pltpu.make_async_copy(slab, xs_ref.at[pl.ds(0, tm * PACK_ROWS), :], sem).wait()


def _dispatch(h2, dest, tm):
    t = h2.shape[0]
    return pl.pallas_call(
        functools.partial(_dispatch_kernel, tm=tm),
        grid=(t // tm,),
        in_specs=[pl.BlockSpec((TOP_K, tm), lambda i: (0, i), memory_space=pltpu.SMEM),
                  pl.BlockSpec((tm, D), lambda i: (i, 0))],
        out_specs=pl.BlockSpec(memory_space=pl.ANY),
        out_shape=jax.ShapeDtypeStruct((t * TOP_K * PACK_ROWS, LANES), jnp.uint32),
        scratch_shapes=[pltpu.VMEM((tm * PACK_ROWS, LANES), jnp.uint32), pltpu.SemaphoreType.DMA],
        compiler_params=_cparams(("arbitrary",)),
        name="dispatch",
    )(dest, h2)


def _gmm_kernel(blk_ref, eid_ref, x_ref, w1_ref, w3_ref, w2_ref, o_ref, *, bm):
    sub = bm // GMM_SPLIT
    for s in range(GMM_SPLIT):
        base = s * sub * PACK_ROWS
        parts = [_unpack_words(_strided(x_ref, base + j, sub)) for j in range(PACK_ROWS)]
        x = jnp.concatenate([p[0] for p in parts] + [p[1] for p in parts], axis=1).astype(BF16)
        a = _silu(jnp.dot(x, w1_ref[...], preferred_element_type=F32)) * jnp.dot(
            x, w3_ref[...], preferred_element_type=F32)
        y = jnp.dot(a.astype(BF16), w2_ref[...], preferred_element_type=F32)
        for j, wj in enumerate(_pack_words(y)):
            o_ref[pl.ds(base + j, sub, stride=PACK_ROWS), :] = wj


def _gmm(xs, blk, item_e, w1, w3, w2, bm):
    n_items = blk.shape[0]
    grid_spec = pltpu.PrefetchScalarGridSpec(
        num_scalar_prefetch=2,
        grid=(n_items,),
        in_specs=[pl.BlockSpec((bm * PACK_ROWS, LANES), lambda w, blk, eid: (blk[w], 0)),
                  pl.BlockSpec((None, D, HID), lambda w, blk, eid: (eid[w], 0, 0)),
                  pl.BlockSpec((None, D, HID), lambda w, blk, eid: (eid[w], 0, 0)),
                  pl.BlockSpec((None, HID, D), lambda w, blk, eid: (eid[w], 0, 0))],
        out_specs=pl.BlockSpec((bm * PACK_ROWS, LANES), lambda w, blk, eid: (w, 0)),
    )
    return pl.pallas_call(
        functools.partial(_gmm_kernel, bm=bm),
        grid_spec=grid_spec,
        out_shape=jax.ShapeDtypeStruct((n_items * bm * PACK_ROWS, LANES), xs.dtype),
        compiler_params=_cparams(("arbitrary",)),
        name="gmm",
    )(blk, item_e, xs, w1, w3, w2)


def _gmm_metadata(counts, n_rows, bm):
    n_blk = n_rows // bm
    n_items = n_blk + N_EXP - 1
    ends = jnp.cumsum(counts)
    starts = ends - counts
    first_blk = starts // bm
    last_blk = jnp.maximum(ends - 1, 0) // bm
    per_e = jnp.where(counts > 0, last_blk - first_blk + 1, 0)
    item_end = jnp.cumsum(per_e)
    item_start = item_end - per_e
    total = item_end[-1]
    w = jnp.minimum(jnp.arange(n_items, dtype=jnp.int32), total - 1)
    e = jnp.sum((item_end[None, :] <= w[:, None]).astype(jnp.int32), axis=1)
    blk = (_lookup(first_blk, e) + w - _lookup(item_start, e)).astype(jnp.int32)
    return blk, e, starts, item_start - first_blk


def _lookup(table, idx):
    hit = idx[..., None] == jnp.arange(table.shape[0], dtype=jnp.int32)
    return jnp.sum(jnp.where(hit, table.astype(jnp.int32), 0), axis=-1)


def _combine_kernel(dest_ref, wt_ref, xsh_ref, mod_ref, fg_ref, ys_ref, o_ref, ybuf, sem, *, tm):
    def fetch(t, carry):
        for k in range(TOP_K):
            pltpu.make_async_copy(_token_rows(ys_ref, dest_ref[k, t]), _token_rows(ybuf, k * tm + t),
                                  sem).start(priority=k % 2)
        return carry

    lax.fori_loop(0, tm, fetch, 0)
    pltpu.make_async_copy(ys_ref.at[pl.ds(0, TOP_K * tm * PACK_ROWS), :], ybuf, sem).wait()

    lo = [jnp.zeros((tm, LANES), F32) for _ in range(PACK_ROWS)]
    hi = [jnp.zeros((tm, LANES), F32) for _ in range(PACK_ROWS)]
    for k in range(TOP_K):
        wk = wt_ref[:, k:k + 1]
        for j in range(PACK_ROWS):
            l, h = _unpack_words(_strided(ybuf, k * tm * PACK_ROWS + j, tm))
            lo[j] = lo[j] + wk * l
            hi[j] = hi[j] + wk * h
    routed = jnp.concatenate(lo + hi, axis=1)
    gate2 = mod_ref[0, 5:6, :]
    x = xsh_ref[...] + gate2 * routed
    ms = jnp.mean(x * x, axis=-1, keepdims=True)
    o_ref[...] = x * lax.rsqrt(ms + EPS) * fg_ref[...]


def _combine(ys, dest, wts_t, xsh, mod3, final_g, seq_len, tm):
    t = xsh.shape[0]
    tiles_per_seq = seq_len // tm
    return pl.pallas_call(
        functools.partial(_combine_kernel, tm=tm),
        grid=(t // tm,),
        in_specs=[pl.BlockSpec((TOP_K, tm), lambda i: (0, i), memory_space=pltpu.SMEM),
                  pl.BlockSpec((tm, TOP_K), lambda i: (i, 0)),
                  pl.BlockSpec((tm, D), lambda i: (i, 0)),
                  pl.BlockSpec((1, 6, D), lambda i: (i // tiles_per_seq, 0, 0)),
                  pl.BlockSpec((1, D), lambda i: (0, 0)),
                  pl.BlockSpec(memory_space=pl.ANY)],
        out_specs=pl.BlockSpec((tm, D), lambda i: (i, 0)),
        out_shape=jax.ShapeDtypeStruct((t, D), F32),
        scratch_shapes=[pltpu.VMEM((TOP_K * tm * PACK_ROWS, LANES), jnp.uint32), pltpu.SemaphoreType.DMA],
        compiler_params=_cparams(("arbitrary",)),
        name="combine",
    )(dest, wts_t, xsh, mod3, final_g, ys)


def _pick(n, cap):
    t = cap
    while n % t:
        t //= 2
    return t


def _encode(x, mod, p):
    nb, seq_len, _ = x.shape
    t = nb * seq_len
    x2 = x.reshape(t, D)
    mod3 = mod.reshape(nb, 6, D)

    proj = _inproj(x2, mod3, p["norm1_g"], p["lower_bounds"], p["w_in"], seq_len, _pick(seq_len, 1024))
    ct = _pick(seq_len, 512)
    o_bw = _hgrn(proj, None, None, seq_len, ct, reverse=True, final=False)
    og = _hgrn(proj, o_bw, p["hg_norm_w"], seq_len, ct, reverse=False, final=True)
    xsh, h2 = _mix(proj, og, x2, mod3, p["conv_w"], p["w_o_hg"], p["w_o_conv"], p["w_out"], p["norm2_g"],
                   p["sh_w1"], p["sh_w3"], p["sh_w2"], seq_len, _pick(seq_len, 512))

    eid, rank, wts, cnt = _route(h2, p["router_wt"], p["router_bias"], _pick(t, 1024))
    counts = cnt[:, 0].astype(jnp.int32)
    bm = _pick(t * TOP_K, GMM_ROWS)
    blk, item_e, starts, item_shift = _gmm_metadata(counts, t * TOP_K, bm)
    dest = _lookup(starts, eid) + rank
    xs = _dispatch(h2, dest, _pick(t, 512))
    ys = _gmm(xs, blk, item_e, p["exp_w1"], p["exp_w3"], p["exp_w2"], bm)
    src = dest + _lookup(item_shift, eid) * bm
    out = _combine(ys, src, wts.T, xsh, mod3, p["final_g"], seq_len, _pick(seq_len, 256))
    return out.reshape(nb, seq_len, D)


def kernel(x_prompt, x_sample, c_prompt, c_sample, w_ada, b_ada, norm1_g, w_in, lower_bounds, hg_norm_w,
           w_o_hg, conv_w, w_o_conv, w_out, norm2_g, router_w, router_bias, exp_w1, exp_w3, exp_w2,
           sh_w1, sh_w3, sh_w2, final_g):
    p = {
        "norm1_g": norm1_g[0].reshape(1, D),
        "lower_bounds": lower_bounds.astype(F32),
        "w_in": w_in[0].astype(BF16),
        "hg_norm_w": hg_norm_w[0].reshape(1, HD),
        "w_o_hg": w_o_hg[0].astype(BF16),
        "conv_w": conv_w[0],
        "w_o_conv": w_o_conv[0].astype(BF16),
        "w_out": w_out[0].astype(BF16),
        "norm2_g": norm2_g[0].reshape(1, D),
        "router_wt": router_w[0].T,
        "router_bias": router_bias[0],
        "exp_w1": exp_w1[0].astype(BF16),
        "exp_w3": exp_w3[0].astype(BF16),
        "exp_w2": exp_w2[0].astype(BF16),
        "sh_w1": sh_w1[0].astype(BF16),
        "sh_w3": sh_w3[0].astype(BF16),
        "sh_w2": sh_w2[0].astype(BF16),
        "final_g": final_g.reshape(1, D),
    }
    nbp = c_prompt.shape[0]
    mod = _ada(jnp.concatenate([c_prompt, c_sample], axis=0), w_ada[0], b_ada[0])
    y_prompt = _encode(x_prompt, mod[:nbp], p)
    y_sample = _encode(x_sample, mod[nbp:], p)
    return (y_prompt, y_sample)
```

```python
import functools

import jax
import jax.numpy as jnp
from jax import lax
from jax.experimental import pallas as pl
from jax.experimental.pallas import tpu as pltpu

F32 = jnp.float32
BF16 = jnp.bfloat16
HIGHEST = lax.Precision.HIGHEST

D = 1024
HEADS = 8
HD = D // HEADS
N_SEG = 10
N_EXP = 64
TOP_K = 8
N_GRP = 8
TOPK_GRP = 4
EXP_PER_GRP = N_EXP // N_GRP
HID = 256
ROUTED_SCALE = 2.5
EPS = 1e-6

CH = 64
UNROLL = 2
EXP2_CLAMP = 115.0
LANES = 128
HALF = D // 2
PACK_ROWS = HALF // LANES
GMM_ROWS = 1024
GMM_SPLIT = 4
VMEM_LIMIT = 56 * 1024 * 1024

S_Q, S_GF, S_GB, S_V, S_OG, S_CB, S_CC, S_CX, S_GA, S_GBM = range(N_SEG)


def _sigmoid(x):
    return jax.nn.sigmoid(x)


def _silu(x):
    return x * jax.nn.sigmoid(x)


def _cparams(sem):
    return pltpu.CompilerParams(dimension_semantics=sem, vmem_limit_bytes=VMEM_LIMIT)


def _ada_kernel(c_ref, w_ref, b_ref, o_ref):
    s = _silu(c_ref[...])
    o_ref[...] = jnp.dot(s, w_ref[...], precision=HIGHEST, preferred_element_type=F32) + b_ref[...]


def _ada(c, w_ada, b_ada):
    nb = c.shape[0]
    return pl.pallas_call(
        _ada_kernel,
        grid=(6,),
        in_specs=[pl.BlockSpec((nb, D), lambda n: (0, 0)),
                  pl.BlockSpec((D, D), lambda n: (0, n)),
                  pl.BlockSpec((1, D), lambda n: (0, n))],
        out_specs=pl.BlockSpec((nb, D), lambda n: (0, n)),
        out_shape=jax.ShapeDtypeStruct((nb, 6 * D), F32),
        compiler_params=_cparams(("arbitrary",)),
        name="ada",
    )(c, w_ada, b_ada.reshape(1, 6 * D))


def _inproj_kernel(x_ref, mod_ref, g1_ref, lbr_ref, w_ref, o_ref, h_sc):
    n = pl.program_id(1)

    @pl.when(n == 0)
    def _():
        x = x_ref[...]
        ms = jnp.mean(x * x, axis=-1, keepdims=True)
        y = x * lax.rsqrt(ms + EPS) * g1_ref[...]
        shift = mod_ref[0, 0:1, :]
        scale = mod_ref[0, 1:2, :]
        h_sc[...] = (y * (1.0 + scale) + shift).astype(BF16)

    def proj():
        return jnp.dot(h_sc[...], w_ref[...], preferred_element_type=F32)

    @pl.when((n == S_Q) | (n == S_OG))
    def _():
        o_ref[...] = _silu(proj()).astype(BF16)

    @pl.when((n == S_GF) | (n == S_GB))
    def _():
        l0 = lbr_ref[0]
        l1 = lbr_ref[1]
        m = jnp.maximum(l0, l1)
        e0 = jnp.exp(l0 - m)
        lb2 = e0 / (e0 + jnp.exp(l1 - m))
        lb = jnp.where(n == S_GB, lb2[1:2, :], lb2[0:1, :])
        o_ref[...] = jnp.log2(lb + (1.0 - lb) * _sigmoid(proj())).astype(BF16)

    @pl.when((n == S_V) | (n == S_CB) | (n == S_CC) | (n == S_CX))
    def _():
        o_ref[...] = proj().astype(BF16)

    @pl.when(n >= S_GA)
    def _():
        o_ref[...] = _sigmoid(proj()).astype(BF16)


def _inproj(x2, mod3, g1, lbr, w_in, seq_len, tm):
    t = x2.shape[0]
    tiles_per_seq = seq_len // tm
    return pl.pallas_call(
        _inproj_kernel,
        grid=(t // tm, N_SEG),
        in_specs=[pl.BlockSpec((tm, D), lambda i, n: (i, 0)),
                  pl.BlockSpec((1, 6, D), lambda i, n: (i // tiles_per_seq, 0, 0)),
                  pl.BlockSpec((1, D), lambda i, n: (0, 0)),
                  pl.BlockSpec((2, 2, D), lambda i, n: (0, 0, 0)),
                  pl.BlockSpec((D, D), lambda i, n: (0, n))],
        out_specs=pl.BlockSpec((None, tm, D), lambda i, n: (n, i, 0)),
        out_shape=jax.ShapeDtypeStruct((N_SEG, t, D), BF16),
        scratch_shapes=[pltpu.VMEM((tm, D), BF16)],
        compiler_params=_cparams(("arbitrary", "arbitrary")),
        name="inproj",
    )(x2, mod3, g1, lbr, w_in)


def _hgrn_kernel(*refs, reverse, final, n_chunks):
    if final:
        q_ref, g_ref, v_ref, og_ref, ob_ref, nw_ref, o_ref, st_sc = refs
    else:
        q_ref, g_ref, v_ref, o_ref, st_sc = refs

    @pl.when(pl.program_id(1) == 0)
    def _():
        st_sc[...] = jnp.zeros_like(st_sc)

    def row_of(p):
        return (CH - 1 - p) if reverse else p

    ti = lax.broadcasted_iota(jnp.int32, (CH, CH), 0)
    si = lax.broadcasted_iota(jnp.int32, (CH, CH), 1)
    if reverse:
        pt, ps = CH - 1 - ti, CH - 1 - si
    else:
        pt, ps = ti, si
    causal = ps <= pt
    tri = causal.astype(BF16)
    m1 = ((pt >= 16) & (pt < 32) & (ps < 16)) | ((pt >= 48) & (ps >= 32) & (ps < 48))
    m2 = causal & ((pt // 16) == (ps // 16))

    ri = lax.broadcasted_iota(jnp.int32, (CH, D), 0)
    pr = (CH - 1 - ri) if reverse else ri
    blk = pr // 16

    def nt(a, bb):
        return lax.dot_general(a, bb, (((1,), (1,)), ((), ())), preferred_element_type=F32)

    def seg(x, p0, p1):
        return x[CH - p1:CH - p0] if reverse else x[p0:p1]

    def place(pieces):
        pieces = sorted(pieces, key=lambda t: -t[0] if reverse else t[0])
        out, pos = [], 0
        for p0, p1, x in pieces:
            lo = (CH - p1) if reverse else p0
            if lo > pos:
                out.append(jnp.zeros((lo - pos, D), BF16))
            out.append(x)
            pos = lo + (p1 - p0)
        if pos < CH:
            out.append(jnp.zeros((CH - pos, D), BF16))
        return jnp.concatenate(out, axis=0)

    def chunk_group(c2, carry):
        for u in range(UNROLL):
            chunk(c2 * UNROLL + u)
        return carry

    def chunk(c):
        cc = (n_chunks - 1 - c) if reverse else c
        rows = pl.ds(pl.multiple_of(cc * CH, CH), CH)
        g16 = g_ref[rows, :]
        b = jnp.dot(tri, g16, preferred_element_type=F32)
        q = q_ref[rows, :].astype(F32)
        kk = 1.0 - jnp.exp2(g16.astype(F32))
        v = v_ref[rows, :]

        def bro(p):
            return b[row_of(p):row_of(p) + 1, :]

        def qside(p0, p1, ref):
            return (p0, p1, (seg(q, p0, p1) * jnp.exp2(seg(b, p0, p1) - ref)).astype(BF16))

        def kside(p0, p1, ref):
            return (p0, p1, (seg(kk, p0, p1) * jnp.exp2(ref - seg(b, p0, p1))).astype(BF16))

        r0 = bro(31)
        qa = place([qside(32, 64, r0)])
        ka = place([kside(0, 32, r0)])
        r1a, r1b = bro(15), bro(47)
        qb = place([qside(16, 32, r1a), qside(48, 64, r1b)])
        kb = place([kside(0, 16, r1a), kside(32, 48, r1b)])
        mid = jnp.where(blk == 0, bro(7), jnp.where(blk == 1, bro(23), jnp.where(blk == 2, bro(39), bro(55))))
        qc = (q * jnp.exp2(jnp.minimum(b - mid, EXP2_CLAMP))).astype(BF16)
        kc = (kk * jnp.exp2(jnp.minimum(mid - b, EXP2_CLAMP))).astype(BF16)
        bl = bro(CH - 1)
        qe = (q * jnp.exp2(b)).astype(BF16)
        kd = (kk * jnp.exp2(bl - b)).astype(BF16)
        dec = jnp.exp2(bl)

        heads = [slice(h * HD, (h + 1) * HD) for h in range(HEADS)]
        att = [(nt(qa[:, c], ka[:, c]) + jnp.where(m1, nt(qb[:, c], kb[:, c]), 0.0)
                + jnp.where(m2, nt(qc[:, c], kc[:, c]), 0.0)).astype(BF16) for c in heads]
        st = [st_sc[h] for h in range(HEADS)]
        inter = [nt(qe[:, c], st[h].astype(BF16)) for h, c in enumerate(heads)]
        upd = [lax.dot_general(v[:, c], kd[:, c], (((0,), (0,)), ((), ())), preferred_element_type=F32)
               for c in heads]
        intra = [jnp.dot(att[h], v[:, c], preferred_element_type=F32) for h, c in enumerate(heads)]
        for h, c in enumerate(heads):
            st_sc[h] = st[h] * dec[:, c] + upd[h]
            o = intra[h] + inter[h]
            if final:
                o = o + ob_ref[rows, c]
                o = o * lax.rsqrt(jnp.mean(o * o, axis=-1, keepdims=True) + EPS)
                o = o * nw_ref[...] * og_ref[rows, c].astype(F32)
                o_ref[rows, c] = o.astype(o_ref.dtype)
            else:
                o_ref[rows, c] = o

    lax.fori_loop(0, n_chunks // UNROLL, chunk_group, 0)


def _hgrn(proj, o_bw, hg_norm_w, seq_len, ct, reverse, final):
    t = proj.shape[1]
    nt_ = seq_len // ct
    nb = t // seq_len

    def tile(b, j):
        return b * nt_ + ((nt_ - 1 - j) if reverse else j)

    def pspec(slot):
        return pl.BlockSpec((None, ct, D), lambda b, j: (slot, tile(b, j), 0))

    in_specs = [pspec(S_Q), pspec(S_GB if reverse else S_GF), pspec(S_V)]
    args = [proj, proj, proj]
    if final:
        in_specs += [pspec(S_OG), pl.BlockSpec((ct, D), lambda b, j: (tile(b, j), 0)),
                     pl.BlockSpec((1, HD), lambda b, j: (0, 0))]
        args += [proj, o_bw, hg_norm_w]
    return pl.pallas_call(
        functools.partial(_hgrn_kernel, reverse=reverse, final=final, n_chunks=ct // CH),
        grid=(nb, nt_),
        in_specs=in_specs,
        out_specs=pl.BlockSpec((ct, D), lambda b, j: (tile(b, j), 0)),
        out_shape=jax.ShapeDtypeStruct((t, D), BF16 if final else F32),
        scratch_shapes=[pltpu.VMEM((HEADS, HD, HD), F32)],
        compiler_params=_cparams(("arbitrary", "arbitrary")),
        name="hgrn_final" if final else "hgrn_bw",
    )(*args)


def _mix_kernel(og_ref, cb_ref, cc_ref, cx_ref, ccp_ref, cxp_ref, ccn_ref, cxn_ref, ga_ref, gb_ref, x_ref,
                mod_ref, cw_ref, wa_ref, wb_ref, wo_ref, g2_ref, s1_ref, s3_ref, s2_ref,
                xsh_ref, h2_ref, *, tiles_per_seq, tm):
    i = pl.program_id(0)
    pos = i % tiles_per_seq
    u = cc_ref[...].astype(F32) * cx_ref[...].astype(F32)
    prev_row = jnp.where(pos == 0, 0.0, ccp_ref[15:16, :].astype(F32) * cxp_ref[15:16, :].astype(F32))
    next_row = jnp.where(pos == tiles_per_seq - 1, 0.0,
                         ccn_ref[0:1, :].astype(F32) * cxn_ref[0:1, :].astype(F32))
    ri = lax.broadcasted_iota(jnp.int32, (tm, D), 0)
    u_prev = jnp.where(ri == 0, prev_row, pltpu.roll(u, 1, axis=0))
    u_next = jnp.where(ri == tm - 1, next_row, pltpu.roll(u, tm - 1, axis=0))
    conv = u_prev * cw_ref[0:1, :] + u * cw_ref[1:2, :] + u_next * cw_ref[2:3, :]
    yb = jnp.dot((cb_ref[...].astype(F32) * conv).astype(BF16), wb_ref[...], preferred_element_type=F32)
    ya = jnp.dot(og_ref[...], wa_ref[...], preferred_element_type=F32)
    merged = ga_ref[...].astype(F32) * ya + gb_ref[...].astype(F32) * yb
    mix = jnp.dot(merged.astype(BF16), wo_ref[...], preferred_element_type=F32)
    gate1 = mod_ref[0, 2:3, :]
    shift2 = mod_ref[0, 3:4, :]
    scale2 = mod_ref[0, 4:5, :]
    gate2 = mod_ref[0, 5:6, :]
    x1 = x_ref[...] + gate1 * mix
    ms = jnp.mean(x1 * x1, axis=-1, keepdims=True)
    h2 = (x1 * lax.rsqrt(ms + EPS) * g2_ref[...]) * (1.0 + scale2) + shift2
    h2_ref[...] = h2
    hb = h2.astype(BF16)
    a = _silu(jnp.dot(hb, s1_ref[...], preferred_element_type=F32)) * jnp.dot(
        hb, s3_ref[...], preferred_element_type=F32)
    shared = jnp.dot(a.astype(BF16), s2_ref[...], preferred_element_type=F32)
    xsh_ref[...] = x1 + gate2 * shared


def _mix(proj, og, x2, mod3, conv_w, w_o_hg, w_o_conv, w_out, g2, s1, s3, s2, seq_len, tm):
    t = x2.shape[0]
    tiles_per_seq = seq_len // tm
    hb = tm // 16
    nhb = t // 16

    def pspec(slot):
        return pl.BlockSpec((None, tm, D), lambda i: (slot, i, 0))

    def halo(slot, nxt):
        if nxt:
            return pl.BlockSpec((None, 16, D), lambda i: (slot, jnp.minimum((i + 1) * hb, nhb - 1), 0))
        return pl.BlockSpec((None, 16, D), lambda i: (slot, jnp.maximum(i * hb - 1, 0), 0))

    def full(shape):
        return pl.BlockSpec(shape, lambda i: (0,) * len(shape))

    in_specs = [pl.BlockSpec((tm, D), lambda i: (i, 0)),
                pspec(S_CB), pspec(S_CC), pspec(S_CX),
                halo(S_CC, False), halo(S_CX, False), halo(S_CC, True), halo(S_CX, True),
                pspec(S_GA), pspec(S_GBM),
                pl.BlockSpec((tm, D), lambda i: (i, 0)),
                pl.BlockSpec((1, 6, D), lambda i: (i // tiles_per_seq, 0, 0)),
                full((3, D)), full((D, D)), full((D, D)), full((D, D)), full((1, D)),
                full((D, HID)), full((D, HID)), full((HID, D))]
    return pl.pallas_call(
        functools.partial(_mix_kernel, tiles_per_seq=tiles_per_seq, tm=tm),
        grid=(t // tm,),
        in_specs=in_specs,
        out_specs=[pl.BlockSpec((tm, D), lambda i: (i, 0)), pl.BlockSpec((tm, D), lambda i: (i, 0))],
        out_shape=[jax.ShapeDtypeStruct((t, D), F32), jax.ShapeDtypeStruct((t, D), F32)],
        compiler_params=_cparams(("arbitrary",)),
        name="mix",
    )(og, proj, proj, proj, proj, proj, proj, proj, proj, proj, x2, mod3, conv_w, w_o_hg, w_o_conv, w_out,
      g2, s1, s3, s2)


def _route_kernel(h_ref, rwt_ref, bias_ref, ustrict_ref, eid_ref, rank_ref, wt_ref, cnt_ref, *, tm):
    @pl.when(pl.program_id(0) == 0)
    def _():
        cnt_ref[...] = jnp.zeros_like(cnt_ref)

    logits = lax.dot_general(rwt_ref[...], h_ref[...], (((1,), (1,)), ((), ())),
                             precision=HIGHEST, preferred_element_type=F32)
    scores = _sigmoid(logits)
    sel = scores + bias_ref[:, 0:1]

    ief = lax.broadcasted_iota(jnp.int32, (N_EXP, tm), 0).astype(F32)
    sel3 = sel.reshape(N_GRP, EXP_PER_GRP, tm)
    i3 = lax.broadcasted_iota(jnp.int32, (N_GRP, EXP_PER_GRP, tm), 1).astype(F32)
    top1 = jnp.max(sel3, axis=1, keepdims=True)
    first = jnp.min(jnp.where(sel3 == top1, i3, float(EXP_PER_GRP)), axis=1, keepdims=True)
    top2 = jnp.max(jnp.where(i3 == first, -jnp.inf, sel3), axis=1, keepdims=True)
    grp = jnp.broadcast_to(top1 + top2, (N_GRP, EXP_PER_GRP, tm)).reshape(N_EXP, tm)
    ig = jnp.floor(ief * (1.0 / EXP_PER_GRP))
    beaten = jnp.zeros((N_EXP, tm), F32)
    for j in range(N_GRP):
        gj = grp[j * EXP_PER_GRP:j * EXP_PER_GRP + 1, :]
        beaten = beaten + jnp.where((gj > grp) | ((gj == grp) & (float(j) < ig)), 1.0, 0.0)
    cand = jnp.where(beaten < float(TOPK_GRP), sel, -jnp.inf)
    chosen = jnp.zeros((N_EXP, tm), F32)
    for _ in range(TOP_K):
        mx = jnp.max(cand, axis=0, keepdims=True)
        idx = jnp.min(jnp.where(cand == mx, ief, float(N_EXP)), axis=0, keepdims=True)
        hit = ief == idx
        chosen = jnp.where(hit, 1.0, chosen)
        cand = jnp.where(hit, -jnp.inf, cand)

    w = scores * chosen
    w = w / jnp.sum(w, axis=0, keepdims=True) * ROUTED_SCALE
    chosen_b = chosen.astype(BF16)
    rank = jnp.dot(chosen_b, ustrict_ref[...], preferred_element_type=F32) + cnt_ref[:, 0:1]
    cnt_ref[...] = cnt_ref[...] + jnp.sum(chosen, axis=1, keepdims=True)
    lstrict = (lax.broadcasted_iota(jnp.int32, (N_EXP, N_EXP), 1)
               < lax.broadcasted_iota(jnp.int32, (N_EXP, N_EXP), 0)).astype(BF16)
    slotpos = jnp.dot(lstrict, chosen_b, preferred_element_type=F32)
    for k in range(TOP_K):
        mk = (chosen > 0.5) & (slotpos == float(k))
        eid_ref[k:k + 1, :] = jnp.sum(jnp.where(mk, ief, 0.0), axis=0, keepdims=True).astype(jnp.int32)
        rank_ref[k:k + 1, :] = jnp.sum(jnp.where(mk, rank, 0.0), axis=0, keepdims=True).astype(jnp.int32)
        wt_ref[k:k + 1, :] = jnp.sum(jnp.where(mk, w, 0.0), axis=0, keepdims=True)


def _route(h2, router_wt, router_bias, tm):
    t = h2.shape[0]
    ustrict = jnp.triu(jnp.ones((tm, tm), BF16), 1)
    bias = jnp.broadcast_to(router_bias.reshape(N_EXP, 1), (N_EXP, LANES))
    kt = lambda i: (0, i)
    return pl.pallas_call(
        functools.partial(_route_kernel, tm=tm),
        grid=(t // tm,),
        in_specs=[pl.BlockSpec((tm, D), lambda i: (i, 0)),
                  pl.BlockSpec((N_EXP, D), lambda i: (0, 0)),
                  pl.BlockSpec((N_EXP, LANES), lambda i: (0, 0)),
                  pl.BlockSpec((tm, tm), lambda i: (0, 0))],
        out_specs=[pl.BlockSpec((TOP_K, tm), kt), pl.BlockSpec((TOP_K, tm), kt),
                   pl.BlockSpec((TOP_K, tm), kt), pl.BlockSpec((N_EXP, LANES), lambda i: (0, 0))],
        out_shape=[jax.ShapeDtypeStruct((TOP_K, t), jnp.int32), jax.ShapeDtypeStruct((TOP_K, t), jnp.int32),
                   jax.ShapeDtypeStruct((TOP_K, t), F32), jax.ShapeDtypeStruct((N_EXP, LANES), F32)],
        compiler_params=_cparams(("arbitrary",)),
        name="route",
    )(h2, router_wt, bias, ustrict)


def _pack_words(x):
    return [pltpu.pack_elementwise([x[:, j * LANES:(j + 1) * LANES], x[:, HALF + j * LANES:HALF + (j + 1) * LANES]],
                                   packed_dtype=BF16) for j in range(PACK_ROWS)]


def _unpack_words(w):
    return tuple(pltpu.unpack_elementwise(w, index=i, packed_dtype=BF16, unpacked_dtype=F32) for i in range(2))


def _token_rows(ref, row):
    return ref.at[pl.ds(pl.multiple_of(row * PACK_ROWS, PACK_ROWS), PACK_ROWS), :]


def _strided(ref, start, m):
    return ref[pl.ds(start, m, stride=PACK_ROWS), :]


def _dispatch_kernel(dest_ref, h_ref, xs_ref, slab, sem, *, tm):
    for j, wj in enumerate(_pack_words(h_ref[...])):
        slab[pl.ds(j, tm, stride=PACK_ROWS), :] = wj

    def send(t, carry):
        src = _token_rows(slab, t)
        for k in range(TOP_K):
            pltpu.make_async_copy(src, _token_rows(xs_ref, dest_ref[k, t]), sem).start(priority=k % 2)
        return carry

    lax.fori_loop(0, tm, send, 0)
    for _ in range(TOP_K):
        pltpu.make_async_copy(slab, xs_ref.at[pl.ds(0, tm * PACK_ROWS), :], sem).wait()


def _dispatch(h2, dest, tm):
    t = h2.shape[0]
    return pl.pallas_call(
        functools.partial(_dispatch_kernel, tm=tm),
        grid=(t // tm,),
        in_specs=[pl.BlockSpec((TOP_K, tm), lambda i: (0, i), memory_space=pltpu.SMEM),
                  pl.BlockSpec((tm, D), lambda i: (i, 0))],
        out_specs=pl.BlockSpec(memory_space=pl.ANY),
        out_shape=jax.ShapeDtypeStruct((t * TOP_K * PACK_ROWS, LANES), jnp.uint32),
        scratch_shapes=[pltpu.VMEM((tm * PACK_ROWS, LANES), jnp.uint32), pltpu.SemaphoreType.DMA],
        compiler_params=_cparams(("arbitrary",)),
        name="dispatch",
    )(dest, h2)


def _gmm_kernel(blk_ref, eid_ref, x_ref, w1_ref, w3_ref, w2_ref, o_ref, *, bm):
    sub = bm // GMM_SPLIT
    for s in range(GMM_SPLIT):
        base = s * sub * PACK_ROWS
        parts = [_unpack_words(_strided(x_ref, base + j, sub)) for j in range(PACK_ROWS)]
        x = jnp.concatenate([p[0] for p in parts] + [p[1] for p in parts], axis=1).astype(BF16)
        a = _silu(jnp.dot(x, w1_ref[...], preferred_element_type=F32)) * jnp.dot(
            x, w3_ref[...], preferred_element_type=F32)
        y = jnp.dot(a.astype(BF16), w2_ref[...], preferred_element_type=F32)
        for j, wj in enumerate(_pack_words(y)):
            o_ref[pl.ds(base + j, sub, stride=PACK_ROWS), :] = wj


def _gmm(xs, blk, item_e, w1, w3, w2, bm):
    n_items = blk.shape[0]
    grid_spec = pltpu.PrefetchScalarGridSpec(
        num_scalar_prefetch=2,
        grid=(n_items,),
        in_specs=[pl.BlockSpec((bm * PACK_ROWS, LANES), lambda w, blk, eid: (blk[w], 0)),
                  pl.BlockSpec((None, D, HID), lambda w, blk, eid: (eid[w], 0, 0)),
                  pl.BlockSpec((None, D, HID), lambda w, blk, eid: (eid[w], 0, 0)),
                  pl.BlockSpec((None, HID, D), lambda w, blk, eid: (eid[w], 0, 0))],
        out_specs=pl.BlockSpec((bm * PACK_ROWS, LANES), lambda w, blk, eid: (w, 0)),
    )
    return pl.pallas_call(
        functools.partial(_gmm_kernel, bm=bm),
        grid_spec=grid_spec,
        out_shape=jax.ShapeDtypeStruct((n_items * bm * PACK_ROWS, LANES), xs.dtype),
        compiler_params=_cparams(("arbitrary",)),
        name="gmm",
    )(blk, item_e, xs, w1, w3, w2)


def _gmm_metadata(counts, n_rows, bm):
    n_blk = n_rows // bm
    n_items = n_blk + N_EXP - 1
    ends = jnp.cumsum(counts)
    starts = ends - counts
    first_blk = starts // bm
    last_blk = jnp.maximum(ends - 1, 0) // bm
    per_e = jnp.where(counts > 0, last_blk - first_blk + 1, 0)
    item_end = jnp.cumsum(per_e)
    item_start = item_end - per_e
    total = item_end[-1]
    w = jnp.minimum(jnp.arange(n_items, dtype=jnp.int32), total - 1)
    e = jnp.sum((item_end[None, :] <= w[:, None]).astype(jnp.int32), axis=1)
    blk = (_lookup(first_blk, e) + w - _lookup(item_start, e)).astype(jnp.int32)
    return blk, e, starts, item_start - first_blk


def _lookup(table, idx):
    hit = idx[..., None] == jnp.arange(table.shape[0], dtype=jnp.int32)
    return jnp.sum(jnp.where(hit, table.astype(jnp.int32), 0), axis=-1)


def _combine_kernel(dest_ref, wt_ref, xsh_ref, mod_ref, fg_ref, ys_ref, o_ref, ybuf, sem, *, tm):
    def fetch(t, carry):
        for k in range(TOP_K):
            pltpu.make_async_copy(_token_rows(ys_ref, dest_ref[k, t]), _token_rows(ybuf, k * tm + t),
                                  sem).start(priority=k % 2)
        return carry

    lax.fori_loop(0, tm, fetch, 0)
    pltpu.make_async_copy(ys_ref.at[pl.ds(0, TOP_K * tm * PACK_ROWS), :], ybuf, sem).wait()

    lo = [jnp.zeros((tm, LANES), F32) for _ in range(PACK_ROWS)]
    hi = [jnp.zeros((tm, LANES), F32) for _ in range(PACK_ROWS)]
    for k in range(TOP_K):
        wk = wt_ref[:, k:k + 1]
        for j in range(PACK_ROWS):
            l, h = _unpack_words(_strided(ybuf, k * tm * PACK_ROWS + j, tm))
            lo[j] = lo[j] + wk * l
            hi[j] = hi[j] + wk * h
    routed = jnp.concatenate(lo + hi, axis=1)
    gate2 = mod_ref[0, 5:6, :]
    x = xsh_ref[...] + gate2 * routed
    ms = jnp.mean(x * x, axis=-1, keepdims=True)
    o_ref[...] = x * lax.rsqrt(ms + EPS) * fg_ref[...]


def _combine(ys, dest, wts_t, xsh, mod3, final_g, seq_len, tm):
    t = xsh.shape[0]
    tiles_per_seq = seq_len // tm
    return pl.pallas_call(
        functools.partial(_combine_kernel, tm=tm),
        grid=(t // tm,),
        in_specs=[pl.BlockSpec((TOP_K, tm), lambda i: (0, i), memory_space=pltpu.SMEM),
                  pl.BlockSpec((tm, TOP_K), lambda i: (i, 0)),
                  pl.BlockSpec((tm, D), lambda i: (i, 0)),
                  pl.BlockSpec((1, 6, D), lambda i: (i // tiles_per_seq, 0, 0)),
                  pl.BlockSpec((1, D), lambda i: (0, 0)),
                  pl.BlockSpec(memory_space=pl.ANY)],
        out_specs=pl.BlockSpec((tm, D), lambda i: (i, 0)),
        out_shape=jax.ShapeDtypeStruct((t, D), F32),
        scratch_shapes=[pltpu.VMEM((TOP_K * tm * PACK_ROWS, LANES), jnp.uint32), pltpu.SemaphoreType.DMA],
        compiler_params=_cparams(("arbitrary",)),
        name="combine",
    )(dest, wts_t, xsh, mod3, final_g, ys)


def _pick(n, cap):
    t = cap
    while n % t:
        t //= 2
    return t


def _encode(x, mod, p):
    nb, seq_len, _ = x.shape
    t = nb * seq_len
    x2 = x.reshape(t, D)
    mod3 = mod.reshape(nb, 6, D)

    proj = _inproj(x2, mod3, p["norm1_g"], p["lower_bounds"], p["w_in"], seq_len, _pick(seq_len, 2048))
    ct = _pick(seq_len, 512)
    o_bw = _hgrn(proj, None, None, seq_len, ct, reverse=True, final=False)
    og = _hgrn(proj, o_bw, p["hg_norm_w"], seq_len, ct, reverse=False, final=True)
    xsh, h2 = _mix(proj, og, x2, mod3, p["conv_w"], p["w_o_hg"], p["w_o_conv"], p["w_out"], p["norm2_g"],
                   p["sh_w1"], p["sh_w3"], p["sh_w2"], seq_len, _pick(seq_len, 512))

    eid, rank, wts, cnt = _route(h2, p["router_wt"], p["router_bias"], _pick(t, 1024))
    counts = cnt[:, 0].astype(jnp.int32)
    bm = _pick(t * TOP_K, GMM_ROWS)
    blk, item_e, starts, item_shift = _gmm_metadata(counts, t * TOP_K, bm)
    dest = _lookup(starts, eid) + rank
    xs = _dispatch(h2, dest, _pick(t, 512))
    ys = _gmm(xs, blk, item_e, p["exp_w1"], p["exp_w3"], p["exp_w2"], bm)
    src = dest + _lookup(item_shift, eid) * bm
    out = _combine(ys, src, wts.T, xsh, mod3, p["final_g"], seq_len, _pick(seq_len, 256))
    return out.reshape(nb, seq_len, D)


def kernel(x_prompt, x_sample, c_prompt, c_sample, w_ada, b_ada, norm1_g, w_in, lower_bounds, hg_norm_w,
           w_o_hg, conv_w, w_o_conv, w_out, norm2_g, router_w, router_bias, exp_w1, exp_w3, exp_w2,
           sh_w1, sh_w3, sh_w2, final_g):
    p = {
        "norm1_g": norm1_g[0].reshape(1, D),
        "lower_bounds": lower_bounds.astype(F32),
        "w_in": w_in[0].astype(BF16),
        "hg_norm_w": hg_norm_w[0].reshape(1, HD),
        "w_o_hg": w_o_hg[0].astype(BF16),
        "conv_w": conv_w[0],
        "w_o_conv": w_o_conv[0].astype(BF16),
        "w_out": w_out[0].astype(BF16),
        "norm2_g": norm2_g[0].reshape(1, D),
        "router_wt": router_w[0].T,
        "router_bias": router_bias[0],
        "exp_w1": exp_w1[0].astype(BF16),
        "exp_w3": exp_w3[0].astype(BF16),
        "exp_w2": exp_w2[0].astype(BF16),
        "sh_w1": sh_w1[0].astype(BF16),
        "sh_w3": sh_w3[0].astype(BF16),
        "sh_w2": sh_w2[0].astype(BF16),
        "final_g": final_g.reshape(1, D),
    }
    nbp = c_prompt.shape[0]
    mod = _ada(jnp.concatenate([c_prompt, c_sample], axis=0), w_ada[0], b_ada[0])
    y_prompt = _encode(x_prompt, mod[:nbp], p)
    y_sample = _encode(x_sample, mod[nbp:], p)
    return (y_prompt, y_sample)
```

```python
import functools

import jax
import jax.numpy as jnp
from jax import lax
from jax.experimental import pallas as pl
from jax.experimental.pallas import tpu as pltpu

F32 = jnp.float32
BF16 = jnp.bfloat16
HIGHEST = lax.Precision.HIGHEST

D = 1024
HEADS = 8
HD = D // HEADS
N_SEG = 10
N_EXP = 64
TOP_K = 8
N_GRP = 8
TOPK_GRP = 4
EXP_PER_GRP = N_EXP // N_GRP
HID = 256
ROUTED_SCALE = 2.5
EPS = 1e-6

CH = 64
UNROLL = 2
EXP2_CLAMP = 115.0
LANES = 128
HALF = D // 2
PACK_ROWS = HALF // LANES
GMM_ROWS = 1024
GMM_SPLIT = 4
VMEM_LIMIT = 56 * 1024 * 1024

S_Q, S_GF, S_GB, S_V, S_OG, S_CB, S_CC, S_CX, S_GA, S_GBM = range(N_SEG)


def _sigmoid(x):
    return 0.5 * jnp.tanh(0.5 * x) + 0.5


def _silu(x):
    return x * _sigmoid(x)


def _cparams(sem):
    return pltpu.CompilerParams(dimension_semantics=sem, vmem_limit_bytes=VMEM_LIMIT)


def _ada_kernel(c_ref, w_ref, b_ref, o_ref):
    s = _silu(c_ref[...])
    o_ref[...] = jnp.dot(s, w_ref[...], precision=HIGHEST, preferred_element_type=F32) + b_ref[...]


def _ada(c, w_ada, b_ada):
    nb = c.shape[0]
    return pl.pallas_call(
        _ada_kernel,
        grid=(6,),
        in_specs=[pl.BlockSpec((nb, D), lambda n: (0, 0)),
                  pl.BlockSpec((D, D), lambda n: (0, n)),
                  pl.BlockSpec((1, D), lambda n: (0, n))],
        out_specs=pl.BlockSpec((nb, D), lambda n: (0, n)),
        out_shape=jax.ShapeDtypeStruct((nb, 6 * D), F32),
        compiler_params=_cparams(("arbitrary",)),
        name="ada",
    )(c, w_ada, b_ada.reshape(1, 6 * D))


def _inproj_kernel(x_ref, mod_ref, g1_ref, lbr_ref, w_ref, o_ref, h_sc):
    n = pl.program_id(1)

    @pl.when(n == 0)
    def _():
        x = x_ref[...]
        ms = jnp.mean(x * x, axis=-1, keepdims=True)
        y = x * lax.rsqrt(ms + EPS) * g1_ref[...]
        shift = mod_ref[0, 0:1, :]
        scale = mod_ref[0, 1:2, :]
        h_sc[...] = (y * (1.0 + scale) + shift).astype(BF16)

    def proj():
        return jnp.dot(h_sc[...], w_ref[...], preferred_element_type=F32)

    @pl.when((n == S_Q) | (n == S_OG))
    def _():
        o_ref[...] = _silu(proj()).astype(BF16)

    @pl.when((n == S_GF) | (n == S_GB))
    def _():
        l0 = lbr_ref[0]
        l1 = lbr_ref[1]
        m = jnp.maximum(l0, l1)
        e0 = jnp.exp(l0 - m)
        lb2 = e0 / (e0 + jnp.exp(l1 - m))
        lb = jnp.where(n == S_GB, lb2[1:2, :], lb2[0:1, :])
        o_ref[...] = jnp.log2(lb + (1.0 - lb) * _sigmoid(proj())).astype(BF16)

    @pl.when((n == S_V) | (n == S_CB) | (n == S_CC) | (n == S_CX))
    def _():
        o_ref[...] = proj().astype(BF16)

    @pl.when(n >= S_GA)
    def _():
        o_ref[...] = _sigmoid(proj()).astype(BF16)


def _inproj(x2, mod3, g1, lbr, w_in, seq_len, tm):
    t = x2.shape[0]
    tiles_per_seq = seq_len // tm
    return pl.pallas_call(
        _inproj_kernel,
        grid=(t // tm, N_SEG),
        in_specs=[pl.BlockSpec((tm, D), lambda i, n: (i, 0)),
                  pl.BlockSpec((1, 6, D), lambda i, n: (i // tiles_per_seq, 0, 0)),
                  pl.BlockSpec((1, D), lambda i, n: (0, 0)),
                  pl.BlockSpec((2, 2, D), lambda i, n: (0, 0, 0)),
                  pl.BlockSpec((D, D), lambda i, n: (0, n))],
        out_specs=pl.BlockSpec((None, tm, D), lambda i, n: (n, i, 0)),
        out_shape=jax.ShapeDtypeStruct((N_SEG, t, D), BF16),
        scratch_shapes=[pltpu.VMEM((tm, D), BF16)],
        compiler_params=_cparams(("arbitrary", "arbitrary")),
        name="inproj",
    )(x2, mod3, g1, lbr, w_in)


def _hgrn_kernel(*refs, reverse, final, n_chunks):
    if final:
        q_ref, g_ref, v_ref, og_ref, ob_ref, nw_ref, o_ref, st_sc = refs
    else:
        q_ref, g_ref, v_ref, o_ref, st_sc = refs

    @pl.when(pl.program_id(1) == 0)
    def _():
        st_sc[...] = jnp.zeros_like(st_sc)

    def row_of(p):
        return (CH - 1 - p) if reverse else p

    ti = lax.broadcasted_iota(jnp.int32, (CH, CH), 0)
    si = lax.broadcasted_iota(jnp.int32, (CH, CH), 1)
    if reverse:
        pt, ps = CH - 1 - ti, CH - 1 - si
    else:
        pt, ps = ti, si
    causal = ps <= pt
    tri = causal.astype(BF16)
    m1 = ((pt >= 16) & (pt < 32) & (ps < 16)) | ((pt >= 48) & (ps >= 32) & (ps < 48))
    m2 = causal & ((pt // 16) == (ps // 16))

    ri = lax.broadcasted_iota(jnp.int32, (CH, D), 0)
    pr = (CH - 1 - ri) if reverse else ri
    blk = pr // 16

    def nt(a, bb):
        return lax.dot_general(a, bb, (((1,), (1,)), ((), ())), preferred_element_type=F32)

    def seg(x, p0, p1):
        return x[CH - p1:CH - p0] if reverse else x[p0:p1]

    def place(pieces):
        pieces = sorted(pieces, key=lambda t: -t[0] if reverse else t[0])
        out, pos = [], 0
        for p0, p1, x in pieces:
            lo = (CH - p1) if reverse else p0
            if lo > pos:
                out.append(jnp.zeros((lo - pos, D), BF16))
            out.append(x)
            pos = lo + (p1 - p0)
        if pos < CH:
            out.append(jnp.zeros((CH - pos, D), BF16))
        return jnp.concatenate(out, axis=0)

    def chunk_group(c2, carry):
        staged = [stage_one(c2 * UNROLL + u) for u in range(UNROLL)]
        for ops in staged:
            stage_two(*ops)
        return carry

    def stage_one(c):
        cc = (n_chunks - 1 - c) if reverse else c
        rows = pl.ds(pl.multiple_of(cc * CH, CH), CH)
        g16 = g_ref[rows, :]
        b = jnp.dot(tri, g16, preferred_element_type=F32)
        q = q_ref[rows, :].astype(F32)
        kk = 1.0 - jnp.exp2(g16.astype(F32))
        v = v_ref[rows, :]

        def bro(p):
            return b[row_of(p):row_of(p) + 1, :]

        def qside(p0, p1, ref):
            return (p0, p1, (seg(q, p0, p1) * jnp.exp2(seg(b, p0, p1) - ref)).astype(BF16))

        def kside(p0, p1, ref):
            return (p0, p1, (seg(kk, p0, p1) * jnp.exp2(ref - seg(b, p0, p1))).astype(BF16))

        r0 = bro(31)
        qa = place([qside(32, 64, r0)])
        ka = place([kside(0, 32, r0)])
        r1a, r1b = bro(15), bro(47)
        qb = place([qside(16, 32, r1a), qside(48, 64, r1b)])
        kb = place([kside(0, 16, r1a), kside(32, 48, r1b)])
        mid = jnp.where(blk == 0, bro(7), jnp.where(blk == 1, bro(23), jnp.where(blk == 2, bro(39), bro(55))))
        qc = (q * jnp.exp2(jnp.minimum(b - mid, EXP2_CLAMP))).astype(BF16)
        kc = (kk * jnp.exp2(jnp.minimum(mid - b, EXP2_CLAMP))).astype(BF16)
        bl = bro(CH - 1)
        qe = (q * jnp.exp2(b)).astype(BF16)
        kd = (kk * jnp.exp2(bl - b)).astype(BF16)
        dec = jnp.exp2(bl)
        return rows, v, qa, ka, qb, kb, qc, kc, qe, kd, dec

    def stage_two(rows, v, qa, ka, qb, kb, qc, kc, qe, kd, dec):
        heads = [slice(h * HD, (h + 1) * HD) for h in range(HEADS)]
        att = [(nt(qa[:, c], ka[:, c]) + jnp.where(m1, nt(qb[:, c], kb[:, c]), 0.0)
                + jnp.where(m2, nt(qc[:, c], kc[:, c]), 0.0)).astype(BF16) for c in heads]
        st = [st_sc[h] for h in range(HEADS)]
        inter = [nt(qe[:, c], st[h].astype(BF16)) for h, c in enumerate(heads)]
        upd = [lax.dot_general(v[:, c], kd[:, c], (((0,), (0,)), ((), ())), preferred_element_type=F32)
               for c in heads]
        intra = [jnp.dot(att[h], v[:, c], preferred_element_type=F32) for h, c in enumerate(heads)]
        for h, c in enumerate(heads):
            st_sc[h] = st[h] * dec[:, c] + upd[h]
            o = intra[h] + inter[h]
            if final:
                o = o + ob_ref[rows, c]
                o = o * lax.rsqrt(jnp.mean(o * o, axis=-1, keepdims=True) + EPS)
                o = o * nw_ref[...] * og_ref[rows, c].astype(F32)
                o_ref[rows, c] = o.astype(o_ref.dtype)
            else:
                o_ref[rows, c] = o

    lax.fori_loop(0, n_chunks // UNROLL, chunk_group, 0)


def _hgrn(proj, o_bw, hg_norm_w, seq_len, ct, reverse, final):
    t = proj.shape[1]
    nt_ = seq_len // ct
    nb = t // seq_len

    def tile(b, j):
        return b * nt_ + ((nt_ - 1 - j) if reverse else j)

    def pspec(slot):
        return pl.BlockSpec((None, ct, D), lambda b, j: (slot, tile(b, j), 0))

    in_specs = [pspec(S_Q), pspec(S_GB if reverse else S_GF), pspec(S_V)]
    args = [proj, proj, proj]
    if final:
        in_specs += [pspec(S_OG), pl.BlockSpec((ct, D), lambda b, j: (tile(b, j), 0)),
                     pl.BlockSpec((1, HD), lambda b, j: (0, 0))]
        args += [proj, o_bw, hg_norm_w]
    return pl.pallas_call(
        functools.partial(_hgrn_kernel, reverse=reverse, final=final, n_chunks=ct // CH),
        grid=(nb, nt_),
        in_specs=in_specs,
        out_specs=pl.BlockSpec((ct, D), lambda b, j: (tile(b, j), 0)),
        out_shape=jax.ShapeDtypeStruct((t, D), BF16 if final else F32),
        scratch_shapes=[pltpu.VMEM((HEADS, HD, HD), F32)],
        compiler_params=_cparams(("arbitrary", "arbitrary")),
        name="hgrn_final" if final else "hgrn_bw",
    )(*args)


def _mix_kernel(og_ref, cb_ref, cc_ref, cx_ref, ccp_ref, cxp_ref, ccn_ref, cxn_ref, ga_ref, gb_ref, x_ref,
                mod_ref, cw_ref, wa_ref, wb_ref, wo_ref, g2_ref, s1_ref, s3_ref, s2_ref,
                xsh_ref, h2_ref, *, tiles_per_seq, tm):
    i = pl.program_id(0)
    pos = i % tiles_per_seq
    u = cc_ref[...].astype(F32) * cx_ref[...].astype(F32)
    prev_row = jnp.where(pos == 0, 0.0, ccp_ref[15:16, :].astype(F32) * cxp_ref[15:16, :].astype(F32))
    next_row = jnp.where(pos == tiles_per_seq - 1, 0.0,
                         ccn_ref[0:1, :].astype(F32) * cxn_ref[0:1, :].astype(F32))
    ri = lax.broadcasted_iota(jnp.int32, (tm, D), 0)
    u_prev = jnp.where(ri == 0, prev_row, pltpu.roll(u, 1, axis=0))
    u_next = jnp.where(ri == tm - 1, next_row, pltpu.roll(u, tm - 1, axis=0))
    conv = u_prev * cw_ref[0:1, :] + u * cw_ref[1:2, :] + u_next * cw_ref[2:3, :]
    yb = jnp.dot((cb_ref[...].astype(F32) * conv).astype(BF16), wb_ref[...], preferred_element_type=F32)
    ya = jnp.dot(og_ref[...], wa_ref[...], preferred_element_type=F32)
    merged = ga_ref[...].astype(F32) * ya + gb_ref[...].astype(F32) * yb
    mix = jnp.dot(merged.astype(BF16), wo_ref[...], preferred_element_type=F32)
    gate1 = mod_ref[0, 2:3, :]
    shift2 = mod_ref[0, 3:4, :]
    scale2 = mod_ref[0, 4:5, :]
    gate2 = mod_ref[0, 5:6, :]
    x1 = x_ref[...] + gate1 * mix
    ms = jnp.mean(x1 * x1, axis=-1, keepdims=True)
    h2 = (x1 * lax.rsqrt(ms + EPS) * g2_ref[...]) * (1.0 + scale2) + shift2
    h2_ref[...] = h2
    hb = h2.astype(BF16)
    a = _silu(jnp.dot(hb, s1_ref[...], preferred_element_type=F32)) * jnp.dot(
        hb, s3_ref[...], preferred_element_type=F32)
    shared = jnp.dot(a.astype(BF16), s2_ref[...], preferred_element_type=F32)
    xsh_ref[...] = x1 + gate2 * shared


def _mix(proj, og, x2, mod3, conv_w, w_o_hg, w_o_conv, w_out, g2, s1, s3, s2, seq_len, tm):
    t = x2.shape[0]
    tiles_per_seq = seq_len // tm
    hb = tm // 16
    nhb = t // 16

    def pspec(slot):
        return pl.BlockSpec((None, tm, D), lambda i: (slot, i, 0))

    def halo(slot, nxt):
        if nxt:
            return pl.BlockSpec((None, 16, D), lambda i: (slot, jnp.minimum((i + 1) * hb, nhb - 1), 0))
        return pl.BlockSpec((None, 16, D), lambda i: (slot, jnp.maximum(i * hb - 1, 0), 0))

    def full(shape):
        return pl.BlockSpec(shape, lambda i: (0,) * len(shape))

    in_specs = [pl.BlockSpec((tm, D), lambda i: (i, 0)),
                pspec(S_CB), pspec(S_CC), pspec(S_CX),
                halo(S_CC, False), halo(S_CX, False), halo(S_CC, True), halo(S_CX, True),
                pspec(S_GA), pspec(S_GBM),
                pl.BlockSpec((tm, D), lambda i: (i, 0)),
                pl.BlockSpec((1, 6, D), lambda i: (i // tiles_per_seq, 0, 0)),
                full((3, D)), full((D, D)), full((D, D)), full((D, D)), full((1, D)),
                full((D, HID)), full((D, HID)), full((HID, D))]
    return pl.pallas_call(
        functools.partial(_mix_kernel, tiles_per_seq=tiles_per_seq, tm=tm),
        grid=(t // tm,),
        in_specs=in_specs,
        out_specs=[pl.BlockSpec((tm, D), lambda i: (i, 0)), pl.BlockSpec((tm, D), lambda i: (i, 0))],
        out_shape=[jax.ShapeDtypeStruct((t, D), F32), jax.ShapeDtypeStruct((t, D), F32)],
        compiler_params=_cparams(("arbitrary",)),
        name="mix",
    )(og, proj, proj, proj, proj, proj, proj, proj, proj, proj, x2, mod3, conv_w, w_o_hg, w_o_conv, w_out,
      g2, s1, s3, s2)


def _route_kernel(h_ref, rwt_ref, bias_ref, ustrict_ref, eid_ref, rank_ref, wt_ref, cnt_ref, *, tm):
    @pl.when(pl.program_id(0) == 0)
    def _():
        cnt_ref[...] = jnp.zeros_like(cnt_ref)

    logits = lax.dot_general(rwt_ref[...], h_ref[...], (((1,), (1,)), ((), ())),
                             precision=HIGHEST, preferred_element_type=F32)
    scores = _sigmoid(logits)
    sel = scores + bias_ref[:, 0:1]

    ief = lax.broadcasted_iota(jnp.int32, (N_EXP, tm), 0).astype(F32)
    sel3 = sel.reshape(N_GRP, EXP_PER_GRP, tm)
    i3 = lax.broadcasted_iota(jnp.int32, (N_GRP, EXP_PER_GRP, tm), 1).astype(F32)
    top1 = jnp.max(sel3, axis=1, keepdims=True)
    first = jnp.min(jnp.where(sel3 == top1, i3, float(EXP_PER_GRP)), axis=1, keepdims=True)
    top2 = jnp.max(jnp.where(i3 == first, -jnp.inf, sel3), axis=1, keepdims=True)
    grp = jnp.broadcast_to(top1 + top2, (N_GRP, EXP_PER_GRP, tm)).reshape(N_EXP, tm)
    ig = jnp.floor(ief * (1.0 / EXP_PER_GRP))
    beaten = jnp.zeros((N_EXP, tm), F32)
    for j in range(N_GRP):
        gj = grp[j * EXP_PER_GRP:j * EXP_PER_GRP + 1, :]
        beaten = beaten + jnp.where((gj > grp) | ((gj == grp) & (float(j) < ig)), 1.0, 0.0)
    cand = jnp.where(beaten < float(TOPK_GRP), sel, -jnp.inf)
    chosen = jnp.zeros((N_EXP, tm), F32)
    for _ in range(TOP_K):
        mx = jnp.max(cand, axis=0, keepdims=True)
        idx = jnp.min(jnp.where(cand == mx, ief, float(N_EXP)), axis=0, keepdims=True)
        hit = ief == idx
        chosen = jnp.where(hit, 1.0, chosen)
        cand = jnp.where(hit, -jnp.inf, cand)

    w = scores * chosen
    w = w / jnp.sum(w, axis=0, keepdims=True) * ROUTED_SCALE
    chosen_b = chosen.astype(BF16)
    rank = jnp.dot(chosen_b, ustrict_ref[...], preferred_element_type=F32) + cnt_ref[:, 0:1]
    cnt_ref[...] = cnt_ref[...] + jnp.sum(chosen, axis=1, keepdims=True)
    lstrict = (lax.broadcasted_iota(jnp.int32, (N_EXP, N_EXP), 1)
               < lax.broadcasted_iota(jnp.int32, (N_EXP, N_EXP), 0)).astype(BF16)
    slotpos = jnp.dot(lstrict, chosen_b, preferred_element_type=F32)
    for k in range(TOP_K):
        mk = (chosen > 0.5) & (slotpos == float(k))
        eid_ref[k:k + 1, :] = jnp.sum(jnp.where(mk, ief, 0.0), axis=0, keepdims=True).astype(jnp.int32)
        rank_ref[k:k + 1, :] = jnp.sum(jnp.where(mk, rank, 0.0), axis=0, keepdims=True).astype(jnp.int32)
        wt_ref[k:k + 1, :] = jnp.sum(jnp.where(mk, w, 0.0), axis=0, keepdims=True)


def _route(h2, router_wt, router_bias, tm):
    t = h2.shape[0]
    ustrict = jnp.triu(jnp.ones((tm, tm), BF16), 1)
    bias = jnp.broadcast_to(router_bias.reshape(N_EXP, 1), (N_EXP, LANES))
    kt = lambda i: (0, i)
    return pl.pallas_call(
        functools.partial(_route_kernel, tm=tm),
        grid=(t // tm,),
        in_specs=[pl.BlockSpec((tm, D), lambda i: (i, 0)),
                  pl.BlockSpec((N_EXP, D), lambda i: (0, 0)),
                  pl.BlockSpec((N_EXP, LANES), lambda i: (0, 0)),
                  pl.BlockSpec((tm, tm), lambda i: (0, 0))],
        out_specs=[pl.BlockSpec((TOP_K, tm), kt), pl.BlockSpec((TOP_K, tm), kt),
                   pl.BlockSpec((TOP_K, tm), kt), pl.BlockSpec((N_EXP, LANES), lambda i: (0, 0))],
        out_shape=[jax.ShapeDtypeStruct((TOP_K, t), jnp.int32), jax.ShapeDtypeStruct((TOP_K, t), jnp.int32),
                   jax.ShapeDtypeStruct((TOP_K, t), F32), jax.ShapeDtypeStruct((N_EXP, LANES), F32)],
        compiler_params=_cparams(("arbitrary",)),
        name="route",
    )(h2, router_wt, bias, ustrict)


def _pack_words(x):
    return [pltpu.pack_elementwise([x[:, j * LANES:(j + 1) * LANES], x[:, HALF + j * LANES:HALF + (j + 1) * LANES]],
                                   packed_dtype=BF16) for j in range(PACK_ROWS)]


def _unpack_words(w):
    return tuple(pltpu.unpack_elementwise(w, index=i, packed_dtype=BF16, unpacked_dtype=F32) for i in range(2))


def _token_rows(ref, row):
    return ref.at[pl.ds(pl.multiple_of(row * PACK_ROWS, PACK_ROWS), PACK_ROWS), :]


def _strided(ref, start, m):
    return ref[pl.ds(start, m, stride=PACK_ROWS), :]


def _dispatch_kernel(dest_ref, h_ref, xs_ref, slab, sem, *, tm):
    for j, wj in enumerate(_pack_words(h_ref[...])):
        slab[pl.ds(j, tm, stride=PACK_ROWS), :] = wj

    def send(t, carry):
        src = _token_rows(slab, t)
        for k in range(TOP_K):
            pltpu.make_async_copy(src, _token_rows(xs_ref, dest_ref[k, t]), sem).start(priority=k % 2)
        return carry

    lax.fori_loop(0, tm, send, 0)
    for _ in range(TOP_K):
        pltpu.make_async_copy(slab, xs_ref.at[pl.ds(0, tm * PACK_ROWS), :], sem).wait()


def _dispatch(h2, dest, tm):
    t = h2.shape[0]
    return pl.pallas_call(
        functools.partial(_dispatch_kernel, tm=tm),
        grid=(t // tm,),
        in_specs=[pl.BlockSpec((TOP_K, tm), lambda i: (0, i), memory_space=pltpu.SMEM),
                  pl.BlockSpec((tm, D), lambda i: (i, 0))],
        out_specs=pl.BlockSpec(memory_space=pl.ANY),
        out_shape=jax.ShapeDtypeStruct((t * TOP_K * PACK_ROWS, LANES), jnp.uint32),
        scratch_shapes=[pltpu.VMEM((tm * PACK_ROWS, LANES), jnp.uint32), pltpu.SemaphoreType.DMA],
        compiler_params=_cparams(("arbitrary",)),
        name="dispatch",
    )(dest, h2)


def _gmm_kernel(blk_ref, eid_ref, x_ref, w1_ref, w3_ref, w2_ref, o_ref, *, bm):
    sub = bm // GMM_SPLIT
    for s in range(GMM_SPLIT):
        base = s * sub * PACK_ROWS
        parts = [_unpack_words(_strided(x_ref, base + j, sub)) for j in range(PACK_ROWS)]
        x = jnp.concatenate([p[0] for p in parts] + [p[1] for p in parts], axis=1).astype(BF16)
        a = _silu(jnp.dot(x, w1_ref[...], preferred_element_type=F32)) * jnp.dot(
            x, w3_ref[...], preferred_element_type=F32)
        y = jnp.dot(a.astype(BF16), w2_ref[...], preferred_element_type=F32)
        for j, wj in enumerate(_pack_words(y)):
            o_ref[pl.ds(base + j, sub, stride=PACK_ROWS), :] = wj


def _gmm(xs, blk, item_e, w1, w3, w2, bm):
    n_items = blk.shape[0]
    grid_spec = pltpu.PrefetchScalarGridSpec(
        num_scalar_prefetch=2,
        grid=(n_items,),
        in_specs=[pl.BlockSpec((bm * PACK_ROWS, LANES), lambda w, blk, eid: (blk[w], 0)),
                  pl.BlockSpec((None, D, HID), lambda w, blk, eid: (eid[w], 0, 0)),
                  pl.BlockSpec((None, D, HID), lambda w, blk, eid: (eid[w], 0, 0)),
                  pl.BlockSpec((None, HID, D), lambda w, blk, eid: (eid[w], 0, 0))],
        out_specs=pl.BlockSpec((bm * PACK_ROWS, LANES), lambda w, blk, eid: (w, 0)),
    )
    return pl.pallas_call(
        functools.partial(_gmm_kernel, bm=bm),
        grid_spec=grid_spec,
        out_shape=jax.ShapeDtypeStruct((n_items * bm * PACK_ROWS, LANES), xs.dtype),
        compiler_params=_cparams(("arbitrary",)),
        name="gmm",
    )(blk, item_e, xs, w1, w3, w2)


def _gmm_metadata(counts, n_rows, bm):
    n_blk = n_rows // bm
    n_items = n_blk + N_EXP - 1
    ends = jnp.cumsum(counts)
    starts = ends - counts
    first_blk = starts // bm
    last_blk = jnp.maximum(ends - 1, 0) // bm
    per_e = jnp.where(counts > 0, last_blk - first_blk + 1, 0)
    item_end = jnp.cumsum(per_e)
    item_start = item_end - per_e
    total = item_end[-1]
    w = jnp.minimum(jnp.arange(n_items, dtype=jnp.int32), total - 1)
    e = jnp.sum((item_end[None, :] <= w[:, None]).astype(jnp.int32), axis=1)
    blk = (_lookup(first_blk, e) + w - _lookup(item_start, e)).astype(jnp.int32)
    return blk, e, starts, item_start - first_blk


def _lookup(table, idx):
    hit = idx[..., None] == jnp.arange(table.shape[0], dtype=jnp.int32)
    return jnp.sum(jnp.where(hit, table.astype(jnp.int32), 0), axis=-1)


def _combine_kernel(dest_ref, wt_ref, xsh_ref, mod_ref, fg_ref, ys_ref, o_ref, ybuf, sem, *, tm):
    def fetch(t, carry):
        for k in range(TOP_K):
            pltpu.make_async_copy(_token_rows(ys_ref, dest_ref[k, t]), _token_rows(ybuf, k * tm + t),
                                  sem).start(priority=k % 2)
        return carry

    lax.fori_loop(0, tm, fetch, 0)
    pltpu.make_async_copy(ys_ref.at[pl.ds(0, TOP_K * tm * PACK_ROWS), :], ybuf, sem).wait()

    lo = [jnp.zeros((tm, LANES), F32) for _ in range(PACK_ROWS)]
    hi = [jnp.zeros((tm, LANES), F32) for _ in range(PACK_ROWS)]
    for k in range(TOP_K):
        wk = wt_ref[:, k:k + 1]
        for j in range(PACK_ROWS):
            l, h = _unpack_words(_strided(ybuf, k * tm * PACK_ROWS + j, tm))
            lo[j] = lo[j] + wk * l
            hi[j] = hi[j] + wk * h
    routed = jnp.concatenate(lo + hi, axis=1)
    gate2 = mod_ref[0, 5:6, :]
    x = xsh_ref[...] + gate2 * routed
    ms = jnp.mean(x * x, axis=-1, keepdims=True)
    o_ref[...] = x * lax.rsqrt(ms + EPS) * fg_ref[...]


def _combine(ys, dest, wts_t, xsh, mod3, final_g, seq_len, tm):
    t = xsh.shape[0]
    tiles_per_seq = seq_len // tm
    return pl.pallas_call(
        functools.partial(_combine_kernel, tm=tm),
        grid=(t // tm,),
        in_specs=[pl.BlockSpec((TOP_K, tm), lambda i: (0, i), memory_space=pltpu.SMEM),
                  pl.BlockSpec((tm, TOP_K), lambda i: (i, 0)),
                  pl.BlockSpec((tm, D), lambda i: (i, 0)),
                  pl.BlockSpec((1, 6, D), lambda i: (i // tiles_per_seq, 0, 0)),
                  pl.BlockSpec((1, D), lambda i: (0, 0)),
                  pl.BlockSpec(memory_space=pl.ANY)],
        out_specs=pl.BlockSpec((tm, D), lambda i: (i, 0)),
        out_shape=jax.ShapeDtypeStruct((t, D), F32),
        scratch_shapes=[pltpu.VMEM((TOP_K * tm * PACK_ROWS, LANES), jnp.uint32), pltpu.SemaphoreType.DMA],
        compiler_params=_cparams(("arbitrary",)),
        name="combine",
    )(dest, wts_t, xsh, mod3, final_g, ys)


def _pick(n, cap):
    t = cap
    while n % t:
        t //= 2
    return t


def _encode(x, mod, p):
    nb, seq_len, _ = x.shape
    t = nb * seq_len
    x2 = x.reshape(t, D)
    mod3 = mod.reshape(nb, 6, D)

    proj = _inproj(x2, mod3, p["norm1_g"], p["lower_bounds"], p["w_in"], seq_len, _pick(seq_len, 2048))
    ct = _pick(seq_len, 512)
    o_bw = _hgrn(proj, None, None, seq_len, ct, reverse=True, final=False)
    og = _hgrn(proj, o_bw, p["hg_norm_w"], seq_len, ct, reverse=False, final=True)
    xsh, h2 = _mix(proj, og, x2, mod3, p["conv_w"], p["w_o_hg"], p["w_o_conv"], p["w_out"], p["norm2_g"],
                   p["sh_w1"], p["sh_w3"], p["sh_w2"], seq_len, _pick(seq_len, 512))

    eid, rank, wts, cnt = _route(h2, p["router_wt"], p["router_bias"], _pick(t, 1024))
    counts = cnt[:, 0].astype(jnp.int32)
    bm = _pick(t * TOP_K, GMM_ROWS)
    blk, item_e, starts, item_shift = _gmm_metadata(counts, t * TOP_K, bm)
    dest = _lookup(starts, eid) + rank
    xs = _dispatch(h2, dest, _pick(t, 512))
    ys = _gmm(xs, blk, item_e, p["exp_w1"], p["exp_w3"], p["exp_w2"], bm)
    src = dest + _lookup(item_shift, eid) * bm
    out = _combine(ys, src, wts.T, xsh, mod3, p["final_g"], seq_len, _pick(seq_len, 256))
    return out.reshape(nb, seq_len, D)


def kernel(x_prompt, x_sample, c_prompt, c_sample, w_ada, b_ada, norm1_g, w_in, lower_bounds, hg_norm_w,
           w_o_hg, conv_w, w_o_conv, w_out, norm2_g, router_w, router_bias, exp_w1, exp_w3, exp_w2,
           sh_w1, sh_w3, sh_w2, final_g):
    p = {
        "norm1_g": norm1_g[0].reshape(1, D),
        "lower_bounds": lower_bounds.astype(F32),
        "w_in": w_in[0].astype(BF16),
        "hg_norm_w": hg_norm_w[0].reshape(1, HD),
        "w_o_hg": w_o_hg[0].astype(BF16),
        "conv_w": conv_w[0],
        "w_o_conv": w_o_conv[0].astype(BF16),
        "w_out": w_out[0].astype(BF16),
        "norm2_g": norm2_g[0].reshape(1, D),
        "router_wt": router_w[0].T,
        "router_bias": router_bias[0],
        "exp_w1": exp_w1[0].astype(BF16),
        "exp_w3": exp_w3[0].astype(BF16),
        "exp_w2": exp_w2[0].astype(BF16),
        "sh_w1": sh_w1[0].astype(BF16),
        "sh_w3": sh_w3[0].astype(BF16),
        "sh_w2": sh_w2[0].astype(BF16),
        "final_g": final_g.reshape(1, D),
    }
    nbp = c_prompt.shape[0]
    mod = _ada(jnp.concatenate([c_prompt, c_sample], axis=0), w_ada[0], b_ada[0])
    y_prompt = _encode(x_prompt, mod[:nbp], p)
    y_sample = _encode(x_sample, mod[nbp:], p)
    return (y_prompt, y_sample)
```

```python
import functools

import jax
import jax.numpy as jnp
from jax import lax
from jax.experimental import pallas as pl
from jax.experimental.pallas import tpu as pltpu

F32 = jnp.float32
BF16 = jnp.bfloat16
HIGHEST = lax.Precision.HIGHEST

D = 1024
HEADS = 8
HD = D // HEADS
N_SEG = 10
N_EXP = 64
TOP_K = 8
N_GRP = 8
TOPK_GRP = 4
EXP_PER_GRP = N_EXP // N_GRP
HID = 256
ROUTED_SCALE = 2.5
EPS = 1e-6

CH = 64
UNROLL = 4
EXP2_CLAMP = 115.0
LANES = 128
HALF = D // 2
PACK_ROWS = HALF // LANES
GMM_ROWS = 1024
GMM_SPLIT = 4
VMEM_LIMIT = 56 * 1024 * 1024

S_Q, S_GF, S_GB, S_V, S_OG, S_CB, S_CC, S_CX, S_GA, S_GBM = range(N_SEG)


def _sigmoid(x):
    return 0.5 * jnp.tanh(0.5 * x) + 0.5


def _silu(x):
    return x * _sigmoid(x)


def _cparams(sem):
    return pltpu.CompilerParams(dimension_semantics=sem, vmem_limit_bytes=VMEM_LIMIT)


def _ada_kernel(c_ref, w_ref, b_ref, o_ref):
    s = _silu(c_ref[...])
    o_ref[...] = jnp.dot(s, w_ref[...], precision=HIGHEST, preferred_element_type=F32) + b_ref[...]


def _ada(c, w_ada, b_ada):
    nb = c.shape[0]
    return pl.pallas_call(
        _ada_kernel,
        grid=(6,),
        in_specs=[pl.BlockSpec((nb, D), lambda n: (0, 0)),
                  pl.BlockSpec((D, D), lambda n: (0, n)),
                  pl.BlockSpec((1, D), lambda n: (0, n))],
        out_specs=pl.BlockSpec((nb, D), lambda n: (0, n)),
        out_shape=jax.ShapeDtypeStruct((nb, 6 * D), F32),
        compiler_params=_cparams(("arbitrary",)),
        name="ada",
    )(c, w_ada, b_ada.reshape(1, 6 * D))


def _inproj_kernel(x_ref, mod_ref, g1_ref, lbr_ref, w_ref, o_ref, h_sc):
    n = pl.program_id(1)

    @pl.when(n == 0)
    def _():
        x = x_ref[...]
        ms = jnp.mean(x * x, axis=-1, keepdims=True)
        y = x * lax.rsqrt(ms + EPS) * g1_ref[...]
        shift = mod_ref[0, 0:1, :]
        scale = mod_ref[0, 1:2, :]
        h_sc[...] = (y * (1.0 + scale) + shift).astype(BF16)

    def proj():
        return jnp.dot(h_sc[...], w_ref[...], preferred_element_type=F32)

    @pl.when((n == S_Q) | (n == S_OG))
    def _():
        o_ref[...] = _silu(proj()).astype(BF16)

    @pl.when((n == S_GF) | (n == S_GB))
    def _():
        l0 = lbr_ref[0]
        l1 = lbr_ref[1]
        m = jnp.maximum(l0, l1)
        e0 = jnp.exp(l0 - m)
        lb2 = e0 / (e0 + jnp.exp(l1 - m))
        lb = jnp.where(n == S_GB, lb2[1:2, :], lb2[0:1, :])
        o_ref[...] = jnp.log2(lb + (1.0 - lb) * _sigmoid(proj())).astype(BF16)

    @pl.when((n == S_V) | (n == S_CB) | (n == S_CC) | (n == S_CX))
    def _():
        o_ref[...] = proj().astype(BF16)

    @pl.when(n >= S_GA)
    def _():
        o_ref[...] = _sigmoid(proj()).astype(BF16)


def _inproj(x2, mod3, g1, lbr, w_in, seq_len, tm):
    t = x2.shape[0]
    tiles_per_seq = seq_len // tm
    return pl.pallas_call(
        _inproj_kernel,
        grid=(t // tm, N_SEG),
        in_specs=[pl.BlockSpec((tm, D), lambda i, n: (i, 0)),
                  pl.BlockSpec((1, 6, D), lambda i, n: (i // tiles_per_seq, 0, 0)),
                  pl.BlockSpec((1, D), lambda i, n: (0, 0)),
                  pl.BlockSpec((2, 2, D), lambda i, n: (0, 0, 0)),
                  pl.BlockSpec((D, D), lambda i, n: (0, n))],
        out_specs=pl.BlockSpec((None, tm, D), lambda i, n: (n, i, 0)),
        out_shape=jax.ShapeDtypeStruct((N_SEG, t, D), BF16),
        scratch_shapes=[pltpu.VMEM((tm, D), BF16)],
        compiler_params=_cparams(("arbitrary", "arbitrary")),
        name="inproj",
    )(x2, mod3, g1, lbr, w_in)


def _hgrn_kernel(*refs, reverse, final, n_chunks):
    if final:
        q_ref, g_ref, v_ref, og_ref, ob_ref, nw_ref, o_ref, st_sc = refs
    else:
        q_ref, g_ref, v_ref, o_ref, st_sc = refs

    @pl.when(pl.program_id(1) == 0)
    def _():
        st_sc[...] = jnp.zeros_like(st_sc)

    def row_of(p):
        return (CH - 1 - p) if reverse else p

    ti = lax.broadcasted_iota(jnp.int32, (CH, CH), 0)
    si = lax.broadcasted_iota(jnp.int32, (CH, CH), 1)
    if reverse:
        pt, ps = CH - 1 - ti, CH - 1 - si
    else:
        pt, ps = ti, si
    causal = ps <= pt
    tri = causal.astype(BF16)
    m1 = ((pt >= 16) & (pt < 32) & (ps < 16)) | ((pt >= 48) & (ps >= 32) & (ps < 48))
    m2 = causal & ((pt // 16) == (ps // 16))

    ri = lax.broadcasted_iota(jnp.int32, (CH, D), 0)
    pr = (CH - 1 - ri) if reverse else ri
    blk = pr // 16

    def nt(a, bb):
        return lax.dot_general(a, bb, (((1,), (1,)), ((), ())), preferred_element_type=F32)

    def seg(x, p0, p1):
        return x[CH - p1:CH - p0] if reverse else x[p0:p1]

    def place(pieces):
        pieces = sorted(pieces, key=lambda t: -t[0] if reverse else t[0])
        out, pos = [], 0
        for p0, p1, x in pieces:
            lo = (CH - p1) if reverse else p0
            if lo > pos:
                out.append(jnp.zeros((lo - pos, D), BF16))
            out.append(x)
            pos = lo + (p1 - p0)
        if pos < CH:
            out.append(jnp.zeros((CH - pos, D), BF16))
        return jnp.concatenate(out, axis=0)

    def chunk_group(c2, carry):
        staged = [stage_one(c2 * UNROLL + u) for u in range(UNROLL)]
        for ops in staged:
            stage_two(*ops)
        return carry

    def stage_one(c):
        cc = (n_chunks - 1 - c) if reverse else c
        rows = pl.ds(pl.multiple_of(cc * CH, CH), CH)
        g16 = g_ref[rows, :]
        b = jnp.dot(tri, g16, preferred_element_type=F32)
        q = q_ref[rows, :].astype(F32)
        kk = 1.0 - jnp.exp2(g16.astype(F32))
        v = v_ref[rows, :]

        def bro(p):
            return b[row_of(p):row_of(p) + 1, :]

        def qside(p0, p1, ref):
            return (p0, p1, (seg(q, p0, p1) * jnp.exp2(seg(b, p0, p1) - ref)).astype(BF16))

        def kside(p0, p1, ref):
            return (p0, p1, (seg(kk, p0, p1) * jnp.exp2(ref - seg(b, p0, p1))).astype(BF16))

        r0 = bro(31)
        qa = place([qside(32, 64, r0)])
        ka = place([kside(0, 32, r0)])
        r1a, r1b = bro(15), bro(47)
        qb = place([qside(16, 32, r1a), qside(48, 64, r1b)])
        kb = place([kside(0, 16, r1a), kside(32, 48, r1b)])
        mid = jnp.where(blk == 0, bro(7), jnp.where(blk == 1, bro(23), jnp.where(blk == 2, bro(39), bro(55))))
        qc = (q * jnp.exp2(jnp.minimum(b - mid, EXP2_CLAMP))).astype(BF16)
        kc = (kk * jnp.exp2(jnp.minimum(mid - b, EXP2_CLAMP))).astype(BF16)
        bl = bro(CH - 1)
        qe = (q * jnp.exp2(b)).astype(BF16)
        kd = (kk * jnp.exp2(bl - b)).astype(BF16)
        dec = jnp.exp2(bl)
        return rows, v, qa, ka, qb, kb, qc, kc, qe, kd, dec

    def stage_two(rows, v, qa, ka, qb, kb, qc, kc, qe, kd, dec):
        heads = [slice(h * HD, (h + 1) * HD) for h in range(HEADS)]
        att = [(nt(qa[:, c], ka[:, c]) + jnp.where(m1, nt(qb[:, c], kb[:, c]), 0.0)
                + jnp.where(m2, nt(qc[:, c], kc[:, c]), 0.0)).astype(BF16) for c in heads]
        st = [st_sc[h] for h in range(HEADS)]
        inter = [nt(qe[:, c], st[h].astype(BF16)) for h, c in enumerate(heads)]
        upd = [lax.dot_general(v[:, c], kd[:, c], (((0,), (0,)), ((), ())), preferred_element_type=F32)
               for c in heads]
        intra = [jnp.dot(att[h], v[:, c], preferred_element_type=F32) for h, c in enumerate(heads)]
        for h, c in enumerate(heads):
            st_sc[h] = st[h] * dec[:, c] + upd[h]
            o = intra[h] + inter[h]
            if final:
                o = o + ob_ref[rows, c]
                o = o * lax.rsqrt(jnp.mean(o * o, axis=-1, keepdims=True) + EPS)
                o = o * nw_ref[...] * og_ref[rows, c].astype(F32)
                o_ref[rows, c] = o.astype(o_ref.dtype)
            else:
                o_ref[rows, c] = o

    lax.fori_loop(0, n_chunks // UNROLL, chunk_group, 0)


def _hgrn(proj, o_bw, hg_norm_w, seq_len, ct, reverse, final):
    t = proj.shape[1]
    nt_ = seq_len // ct
    nb = t // seq_len
    assert ct % (CH * UNROLL) == 0, (ct, CH, UNROLL)

    def tile(b, j):
        return b * nt_ + ((nt_ - 1 - j) if reverse else j)

    def pspec(slot):
        return pl.BlockSpec((None, ct, D), lambda b, j: (slot, tile(b, j), 0))

    in_specs = [pspec(S_Q), pspec(S_GB if reverse else S_GF), pspec(S_V)]
    args = [proj, proj, proj]
    if final:
        in_specs += [pspec(S_OG), pl.BlockSpec((ct, D), lambda b, j: (tile(b, j), 0)),
                     pl.BlockSpec((1, HD), lambda b, j: (0, 0))]
        args += [proj, o_bw, hg_norm_w]
    return pl.pallas_call(
        functools.partial(_hgrn_kernel, reverse=reverse, final=final, n_chunks=ct // CH),
        grid=(nb, nt_),
        in_specs=in_specs,
        out_specs=pl.BlockSpec((ct, D), lambda b, j: (tile(b, j), 0)),
        out_shape=jax.ShapeDtypeStruct((t, D), BF16 if final else F32),
        scratch_shapes=[pltpu.VMEM((HEADS, HD, HD), F32)],
        compiler_params=_cparams(("arbitrary", "arbitrary")),
        name="hgrn_final" if final else "hgrn_bw",
    )(*args)


def _mix_kernel(og_ref, cb_ref, cc_ref, cx_ref, ccp_ref, cxp_ref, ccn_ref, cxn_ref, ga_ref, gb_ref, x_ref,
                mod_ref, cw_ref, wa_ref, wb_ref, wo_ref, g2_ref, s1_ref, s3_ref, s2_ref,
                xsh_ref, h2_ref, *, tiles_per_seq, tm):
    i = pl.program_id(0)
    pos = i % tiles_per_seq
    u = cc_ref[...].astype(F32) * cx_ref[...].astype(F32)
    prev_row = jnp.where(pos == 0, 0.0, ccp_ref[15:16, :].astype(F32) * cxp_ref[15:16, :].astype(F32))
    next_row = jnp.where(pos == tiles_per_seq - 1, 0.0,
                         ccn_ref[0:1, :].astype(F32) * cxn_ref[0:1, :].astype(F32))
    ri = lax.broadcasted_iota(jnp.int32, (tm, D), 0)
    u_prev = jnp.where(ri == 0, prev_row, pltpu.roll(u, 1, axis=0))
    u_next = jnp.where(ri == tm - 1, next_row, pltpu.roll(u, tm - 1, axis=0))
    conv = u_prev * cw_ref[0:1, :] + u * cw_ref[1:2, :] + u_next * cw_ref[2:3, :]
    yb = jnp.dot((cb_ref[...].astype(F32) * conv).astype(BF16), wb_ref[...], preferred_element_type=F32)
    ya = jnp.dot(og_ref[...], wa_ref[...], preferred_element_type=F32)
    merged = ga_ref[...].astype(F32) * ya + gb_ref[...].astype(F32) * yb
    mix = jnp.dot(merged.astype(BF16), wo_ref[...], preferred_element_type=F32)
    gate1 = mod_ref[0, 2:3, :]
    shift2 = mod_ref[0, 3:4, :]
    scale2 = mod_ref[0, 4:5, :]
    gate2 = mod_ref[0, 5:6, :]
    x1 = x_ref[...] + gate1 * mix
    ms = jnp.mean(x1 * x1, axis=-1, keepdims=True)
    h2 = (x1 * lax.rsqrt(ms + EPS) * g2_ref[...]) * (1.0 + scale2) + shift2
    h2_ref[...] = h2
    hb = h2.astype(BF16)
    a = _silu(jnp.dot(hb, s1_ref[...], preferred_element_type=F32)) * jnp.dot(
        hb, s3_ref[...], preferred_element_type=F32)
    shared = jnp.dot(a.astype(BF16), s2_ref[...], preferred_element_type=F32)
    xsh_ref[...] = x1 + gate2 * shared


def _mix(proj, og, x2, mod3, conv_w, w_o_hg, w_o_conv, w_out, g2, s1, s3, s2, seq_len, tm):
    t = x2.shape[0]
    tiles_per_seq = seq_len // tm
    hb = tm // 16
    nhb = t // 16

    def pspec(slot):
        return pl.BlockSpec((None, tm, D), lambda i: (slot, i, 0))

    def halo(slot, nxt):
        if nxt:
            return pl.BlockSpec((None, 16, D), lambda i: (slot, jnp.minimum((i + 1) * hb, nhb - 1), 0))
        return pl.BlockSpec((None, 16, D), lambda i: (slot, jnp.maximum(i * hb - 1, 0), 0))

    def full(shape):
        return pl.BlockSpec(shape, lambda i: (0,) * len(shape))

    in_specs = [pl.BlockSpec((tm, D), lambda i: (i, 0)),
                pspec(S_CB), pspec(S_CC), pspec(S_CX),
                halo(S_CC, False), halo(S_CX, False), halo(S_CC, True), halo(S_CX, True),
                pspec(S_GA), pspec(S_GBM),
                pl.BlockSpec((tm, D), lambda i: (i, 0)),
                pl.BlockSpec((1, 6, D), lambda i: (i // tiles_per_seq, 0, 0)),
                full((3, D)), full((D, D)), full((D, D)), full((D, D)), full((1, D)),
                full((D, HID)), full((D, HID)), full((HID, D))]
    return pl.pallas_call(
        functools.partial(_mix_kernel, tiles_per_seq=tiles_per_seq, tm=tm),
        grid=(t // tm,),
        in_specs=in_specs,
        out_specs=[pl.BlockSpec((tm, D), lambda i: (i, 0)), pl.BlockSpec((tm, D), lambda i: (i, 0))],
        out_shape=[jax.ShapeDtypeStruct((t, D), F32), jax.ShapeDtypeStruct((t, D), F32)],
        compiler_params=_cparams(("arbitrary",)),
        name="mix",
    )(og, proj, proj, proj, proj, proj, proj, proj, proj, proj, x2, mod3, conv_w, w_o_hg, w_o_conv, w_out,
      g2, s1, s3, s2)


def _route_kernel(h_ref, rwt_ref, bias_ref, ustrict_ref, eid_ref, rank_ref, wt_ref, cnt_ref, *, tm):
    @pl.when(pl.program_id(0) == 0)
    def _():
        cnt_ref[...] = jnp.zeros_like(cnt_ref)

    logits = lax.dot_general(rwt_ref[...], h_ref[...], (((1,), (1,)), ((), ())),
                             precision=HIGHEST, preferred_element_type=F32)
    scores = _sigmoid(logits)
    sel = scores + bias_ref[:, 0:1]

    ief = lax.broadcasted_iota(jnp.int32, (N_EXP, tm), 0).astype(F32)
    sel3 = sel.reshape(N_GRP, EXP_PER_GRP, tm)
    i3 = lax.broadcasted_iota(jnp.int32, (N_GRP, EXP_PER_GRP, tm), 1).astype(F32)
    top1 = jnp.max(sel3, axis=1, keepdims=True)
    first = jnp.min(jnp.where(sel3 == top1, i3, float(EXP_PER_GRP)), axis=1, keepdims=True)
    top2 = jnp.max(jnp.where(i3 == first, -jnp.inf, sel3), axis=1, keepdims=True)
    grp = jnp.broadcast_to(top1 + top2, (N_GRP, EXP_PER_GRP, tm)).reshape(N_EXP, tm)
    ig = jnp.floor(ief * (1.0 / EXP_PER_GRP))
    beaten = jnp.zeros((N_EXP, tm), F32)
    for j in range(N_GRP):
        gj = grp[j * EXP_PER_GRP:j * EXP_PER_GRP + 1, :]
        beaten = beaten + jnp.where((gj > grp) | ((gj == grp) & (float(j) < ig)), 1.0, 0.0)
    cand = jnp.where(beaten < float(TOPK_GRP), sel, -jnp.inf)
    chosen = jnp.zeros((N_EXP, tm), F32)
    for _ in range(TOP_K):
        mx = jnp.max(cand, axis=0, keepdims=True)
        idx = jnp.min(jnp.where(cand == mx, ief, float(N_EXP)), axis=0, keepdims=True)
        hit = ief == idx
        chosen = jnp.where(hit, 1.0, chosen)
        cand = jnp.where(hit, -jnp.inf, cand)

    w = scores * chosen
    w = w / jnp.sum(w, axis=0, keepdims=True) * ROUTED_SCALE
    chosen_b = chosen.astype(BF16)
    rank = jnp.dot(chosen_b, ustrict_ref[...], preferred_element_type=F32) + cnt_ref[:, 0:1]
    cnt_ref[...] = cnt_ref[...] + jnp.sum(chosen, axis=1, keepdims=True)
    lstrict = (lax.broadcasted_iota(jnp.int32, (N_EXP, N_EXP), 1)
               < lax.broadcasted_iota(jnp.int32, (N_EXP, N_EXP), 0)).astype(BF16)
    slotpos = jnp.dot(lstrict, chosen_b, preferred_element_type=F32)
    for k in range(TOP_K):
        mk = (chosen > 0.5) & (slotpos == float(k))
        eid_ref[k:k + 1, :] = jnp.sum(jnp.where(mk, ief, 0.0), axis=0, keepdims=True).astype(jnp.int32)
        rank_ref[k:k + 1, :] = jnp.sum(jnp.where(mk, rank, 0.0), axis=0, keepdims=True).astype(jnp.int32)
        wt_ref[k:k + 1, :] = jnp.sum(jnp.where(mk, w, 0.0), axis=0, keepdims=True)


def _route(h2, router_wt, router_bias, tm):
    t = h2.shape[0]
    ustrict = jnp.triu(jnp.ones((tm, tm), BF16), 1)
    bias = jnp.broadcast_to(router_bias.reshape(N_EXP, 1), (N_EXP, LANES))
    kt = lambda i: (0, i)
    return pl.pallas_call(
        functools.partial(_route_kernel, tm=tm),
        grid=(t // tm,),
        in_specs=[pl.BlockSpec((tm, D), lambda i: (i, 0)),
                  pl.BlockSpec((N_EXP, D), lambda i: (0, 0)),
                  pl.BlockSpec((N_EXP, LANES), lambda i: (0, 0)),
                  pl.BlockSpec((tm, tm), lambda i: (0, 0))],
        out_specs=[pl.BlockSpec((TOP_K, tm), kt), pl.BlockSpec((TOP_K, tm), kt),
                   pl.BlockSpec((TOP_K, tm), kt), pl.BlockSpec((N_EXP, LANES), lambda i: (0, 0))],
        out_shape=[jax.ShapeDtypeStruct((TOP_K, t), jnp.int32), jax.ShapeDtypeStruct((TOP_K, t), jnp.int32),
                   jax.ShapeDtypeStruct((TOP_K, t), F32), jax.ShapeDtypeStruct((N_EXP, LANES), F32)],
        compiler_params=_cparams(("arbitrary",)),
        name="route",
    )(h2, router_wt, bias, ustrict)


def _pack_words(x):
    return [pltpu.pack_elementwise([x[:, j * LANES:(j + 1) * LANES], x[:, HALF + j * LANES:HALF + (j + 1) * LANES]],
                                   packed_dtype=BF16) for j in range(PACK_ROWS)]


def _unpack_words(w):
    return tuple(pltpu.unpack_elementwise(w, index=i, packed_dtype=BF16, unpacked_dtype=F32) for i in range(2))


def _token_rows(ref, row):
    return ref.at[pl.ds(pl.multiple_of(row * PACK_ROWS, PACK_ROWS), PACK_ROWS), :]


def _strided(ref, start, m):
    return ref[pl.ds(start, m, stride=PACK_ROWS), :]


def _dispatch_kernel(dest_ref, h_ref, xs_ref, slab, sem, *, tm):
    for j, wj in enumerate(_pack_words(h_ref[...])):
        slab[pl.ds(j, tm, stride=PACK_ROWS), :] = wj

    def send(t, carry):
        src = _token_rows(slab, t)
        for k in range(TOP_K):
            pltpu.make_async_copy(src, _token_rows(xs_ref, dest_ref[k, t]), sem).start(priority=k % 2)
        return carry

    lax.fori_loop(0, tm, send, 0)
    for _ in range(TOP_K):
        pltpu.make_async_copy(slab, xs_ref.at[pl.ds(0, tm * PACK_ROWS), :], sem).wait()


def _dispatch(h2, dest, tm):
    t = h2.shape[0]
    return pl.pallas_call(
        functools.partial(_dispatch_kernel, tm=tm),
        grid=(t // tm,),
        in_specs=[pl.BlockSpec((TOP_K, tm), lambda i: (0, i), memory_space=pltpu.SMEM),
                  pl.BlockSpec((tm, D), lambda i: (i, 0))],
        out_specs=pl.BlockSpec(memory_space=pl.ANY),
        out_shape=jax.ShapeDtypeStruct((t * TOP_K * PACK_ROWS, LANES), jnp.uint32),
        scratch_shapes=[pltpu.VMEM((tm * PACK_ROWS, LANES), jnp.uint32), pltpu.SemaphoreType.DMA],
        compiler_params=_cparams(("arbitrary",)),
        name="dispatch",
    )(dest, h2)


def _gmm_kernel(blk_ref, eid_ref, x_ref, w1_ref, w3_ref, w2_ref, o_ref, *, bm):
    sub = bm // GMM_SPLIT
    for s in range(GMM_SPLIT):
        base = s * sub * PACK_ROWS
        parts = [_unpack_words(_strided(x_ref, base + j, sub)) for j in range(PACK_ROWS)]
        x = jnp.concatenate([p[0] for p in parts] + [p[1] for p in parts], axis=1).astype(BF16)
        a = _silu(jnp.dot(x, w1_ref[...], preferred_element_type=F32)) * jnp.dot(
            x, w3_ref[...], preferred_element_type=F32)
        y = jnp.dot(a.astype(BF16), w2_ref[...], preferred_element_type=F32)
        for j, wj in enumerate(_pack_words(y)):
            o_ref[pl.ds(base + j, sub, stride=PACK_ROWS), :] = wj


def _gmm(xs, blk, item_e, w1, w3, w2, bm):
    n_items = blk.shape[0]
    grid_spec = pltpu.PrefetchScalarGridSpec(
        num_scalar_prefetch=2,
        grid=(n_items,),
        in_specs=[pl.BlockSpec((bm * PACK_ROWS, LANES), lambda w, blk, eid: (blk[w], 0)),
                  pl.BlockSpec((None, D, HID), lambda w, blk, eid: (eid[w], 0, 0)),
                  pl.BlockSpec((None, D, HID), lambda w, blk, eid: (eid[w], 0, 0)),
                  pl.BlockSpec((None, HID, D), lambda w, blk, eid: (eid[w], 0, 0))],
        out_specs=pl.BlockSpec((bm * PACK_ROWS, LANES), lambda w, blk, eid: (w, 0)),
    )
    return pl.pallas_call(
        functools.partial(_gmm_kernel, bm=bm),
        grid_spec=grid_spec,
        out_shape=jax.ShapeDtypeStruct((n_items * bm * PACK_ROWS, LANES), xs.dtype),
        compiler_params=_cparams(("arbitrary",)),
        name="gmm",
    )(blk, item_e, xs, w1, w3, w2)


def _gmm_metadata(counts, n_rows, bm):
    n_blk = n_rows // bm
    n_items = n_blk + N_EXP - 1
    ends = jnp.cumsum(counts)
    starts = ends - counts
    first_blk = starts // bm
    last_blk = jnp.maximum(ends - 1, 0) // bm
    per_e = jnp.where(counts > 0, last_blk - first_blk + 1, 0)
    item_end = jnp.cumsum(per_e)
    item_start = item_end - per_e
    total = item_end[-1]
    w = jnp.minimum(jnp.arange(n_items, dtype=jnp.int32), total - 1)
    e = jnp.sum((item_end[None, :] <= w[:, None]).astype(jnp.int32), axis=1)
    blk = (_lookup(first_blk, e) + w - _lookup(item_start, e)).astype(jnp.int32)
    return blk, e, starts, item_start - first_blk


def _lookup(table, idx):
    hit = idx[..., None] == jnp.arange(table.shape[0], dtype=jnp.int32)
    return jnp.sum(jnp.where(hit, table.astype(jnp.int32), 0), axis=-1)


def _combine_kernel(dest_ref, next_ref, wt_ref, xsh_ref, mod_ref, fg_ref, ys_ref, o_ref, ybuf, sem, *, tm, n_steps):
    i = pl.program_id(0)
    slot = i % 2
    rows_per_slot = TOP_K * tm

    def fetch(idx_ref, into):
        def body(t, carry):
            for k in range(TOP_K):
                pltpu.make_async_copy(_token_rows(ys_ref, idx_ref[k, t]),
                                      _token_rows(ybuf, into * rows_per_slot + k * tm + t),
                                      sem.at[into]).start(priority=k % 2)
            return carry
        lax.fori_loop(0, tm, body, 0)

    @pl.when(i == 0)
    def _():
        fetch(dest_ref, slot)

    @pl.when(i + 1 < n_steps)
    def _():
        fetch(next_ref, 1 - slot)

    base = pl.multiple_of(slot * (rows_per_slot * PACK_ROWS), rows_per_slot * PACK_ROWS)
    half = ybuf.at[pl.ds(base, rows_per_slot * PACK_ROWS), :]
    pltpu.make_async_copy(ys_ref.at[pl.ds(0, rows_per_slot * PACK_ROWS), :], half, sem.at[slot]).wait()

    lo = [jnp.zeros((tm, LANES), F32) for _ in range(PACK_ROWS)]
    hi = [jnp.zeros((tm, LANES), F32) for _ in range(PACK_ROWS)]
    for k in range(TOP_K):
        wk = wt_ref[:, k:k + 1]
        for j in range(PACK_ROWS):
            l, h = _unpack_words(ybuf[pl.ds(base + k * tm * PACK_ROWS + j, tm, stride=PACK_ROWS), :])
            lo[j] = lo[j] + wk * l
            hi[j] = hi[j] + wk * h
    routed = jnp.concatenate(lo + hi, axis=1)
    gate2 = mod_ref[0, 5:6, :]
    x = xsh_ref[...] + gate2 * routed
    ms = jnp.mean(x * x, axis=-1, keepdims=True)
    o_ref[...] = x * lax.rsqrt(ms + EPS) * fg_ref[...]


def _combine(ys, src, wts_t, xsh, mod3, final_g, seq_len, tm):
    t = xsh.shape[0]
    tiles_per_seq = seq_len // tm
    n_steps = t // tm
    return pl.pallas_call(
        functools.partial(_combine_kernel, tm=tm, n_steps=n_steps),
        grid=(n_steps,),
        in_specs=[pl.BlockSpec((TOP_K, tm), lambda i: (0, i), memory_space=pltpu.SMEM),
                  pl.BlockSpec((TOP_K, tm), lambda i: (0, jnp.minimum(i + 1, n_steps - 1)),
                               memory_space=pltpu.SMEM),
                  pl.BlockSpec((tm, TOP_K), lambda i: (i, 0)),
                  pl.BlockSpec((tm, D), lambda i: (i, 0)),
                  pl.BlockSpec((1, 6, D), lambda i: (i // tiles_per_seq, 0, 0)),
                  pl.BlockSpec((1, D), lambda i: (0, 0)),
                  pl.BlockSpec(memory_space=pl.ANY)],
        out_specs=pl.BlockSpec((tm, D), lambda i: (i, 0)),
        out_shape=jax.ShapeDtypeStruct((t, D), F32),
        scratch_shapes=[pltpu.VMEM((2 * TOP_K * tm * PACK_ROWS, LANES), jnp.uint32),
                        pltpu.SemaphoreType.DMA((2,))],
        compiler_params=_cparams(("arbitrary",)),
        name="combine",
    )(src, src, wts_t, xsh, mod3, final_g, ys)


def _pick(n, cap):
    t = cap
    while n % t:
        t //= 2
    return t


def _encode(x, mod, p):
    nb, seq_len, _ = x.shape
    t = nb * seq_len
    x2 = x.reshape(t, D)
    mod3 = mod.reshape(nb, 6, D)

    proj = _inproj(x2, mod3, p["norm1_g"], p["lower_bounds"], p["w_in"], seq_len, _pick(seq_len, 2048))
    ct = _pick(seq_len, 512)
    o_bw = _hgrn(proj, None, None, seq_len, ct, reverse=True, final=False)
    og = _hgrn(proj, o_bw, p["hg_norm_w"], seq_len, ct, reverse=False, final=True)
    xsh, h2 = _mix(proj, og, x2, mod3, p["conv_w"], p["w_o_hg"], p["w_o_conv"], p["w_out"], p["norm2_g"],
                   p["sh_w1"], p["sh_w3"], p["sh_w2"], seq_len, _pick(seq_len, 512))

    eid, rank, wts, cnt = _route(h2, p["router_wt"], p["router_bias"], _pick(t, 1024))
    counts = cnt[:, 0].astype(jnp.int32)
    bm = _pick(t * TOP_K, GMM_ROWS)
    blk, item_e, starts, item_shift = _gmm_metadata(counts, t * TOP_K, bm)
    dest = _lookup(starts, eid) + rank
    xs = _dispatch(h2, dest, _pick(t, 512))
    ys = _gmm(xs, blk, item_e, p["exp_w1"], p["exp_w3"], p["exp_w2"], bm)
    src = dest + _lookup(item_shift, eid) * bm
    out = _combine(ys, src, wts.T, xsh, mod3, p["final_g"], seq_len, _pick(seq_len, 256))
    return out.reshape(nb, seq_len, D)


def kernel(x_prompt, x_sample, c_prompt, c_sample, w_ada, b_ada, norm1_g, w_in, lower_bounds, hg_norm_w,
           w_o_hg, conv_w, w_o_conv, w_out, norm2_g, router_w, router_bias, exp_w1, exp_w3, exp_w2,
           sh_w1, sh_w3, sh_w2, final_g):
    p = {
        "norm1_g": norm1_g[0].reshape(1, D),
        "lower_bounds": lower_bounds.astype(F32),
        "w_in": w_in[0].astype(BF16),
        "hg_norm_w": hg_norm_w[0].reshape(1, HD),
        "w_o_hg": w_o_hg[0].astype(BF16),
        "conv_w": conv_w[0],
        "w_o_conv": w_o_conv[0].astype(BF16),
        "w_out": w_out[0].astype(BF16),
        "norm2_g": norm2_g[0].reshape(1, D),
        "router_wt": router_w[0].T,
        "router_bias": router_bias[0],
        "exp_w1": exp_w1[0].astype(BF16),
        "exp_w3": exp_w3[0].astype(BF16),
        "exp_w2": exp_w2[0].astype(BF16),
        "sh_w1": sh_w1[0].astype(BF16),
        "sh_w3": sh_w3[0].astype(BF16),
        "sh_w2": sh_w2[0].astype(BF16),
        "final_g": final_g.reshape(1, D),
    }
    nbp = c_prompt.shape[0]
    mod = _ada(jnp.concatenate([c_prompt, c_sample], axis=0), w_ada[0], b_ada[0])
    y_prompt = _encode(x_prompt, mod[:nbp], p)
    y_sample = _encode(x_sample, mod[nbp:], p)
    return (y_prompt, y_sample)
```

```python
import functools

import jax
import jax.numpy as jnp
from jax import lax
from jax.experimental import pallas as pl
from jax.experimental.pallas import tpu as pltpu

F32 = jnp.float32
BF16 = jnp.bfloat16
HIGHEST = lax.Precision.HIGHEST

D = 1024
HEADS = 8
HD = D // HEADS
N_SEG = 10
N_EXP = 64
TOP_K = 8
N_GRP = 8
TOPK_GRP = 4
EXP_PER_GRP = N_EXP // N_GRP
HID = 256
ROUTED_SCALE = 2.5
EPS = 1e-6

CH = 64
UNROLL = 4
EXP2_CLAMP = 115.0
LANES = 128
HALF = D // 2
PACK_ROWS = HALF // LANES
GMM_ROWS = 1024
GMM_SPLIT = 4
VMEM_LIMIT = 56 * 1024 * 1024

S_Q, S_GF, S_GB, S_V, S_OG, S_CB, S_CC, S_CX, S_GA, S_GBM = range(N_SEG)


def _sigmoid(x):
    return 0.5 * jnp.tanh(0.5 * x) + 0.5


def _silu(x):
    return x * _sigmoid(x)


def _cparams(sem):
    return pltpu.CompilerParams(dimension_semantics=sem, vmem_limit_bytes=VMEM_LIMIT)


def _ada_kernel(c_ref, w_ref, b_ref, o_ref):
    s = _silu(c_ref[...])
    o_ref[...] = jnp.dot(s, w_ref[...], precision=HIGHEST, preferred_element_type=F32) + b_ref[...]


def _ada(c, w_ada, b_ada):
    nb = c.shape[0]
    return pl.pallas_call(
        _ada_kernel,
        grid=(6,),
        in_specs=[pl.BlockSpec((nb, D), lambda n: (0, 0)),
                  pl.BlockSpec((D, D), lambda n: (0, n)),
                  pl.BlockSpec((1, D), lambda n: (0, n))],
        out_specs=pl.BlockSpec((nb, D), lambda n: (0, n)),
        out_shape=jax.ShapeDtypeStruct((nb, 6 * D), F32),
        compiler_params=_cparams(("arbitrary",)),
        name="ada",
    )(c, w_ada, b_ada.reshape(1, 6 * D))


def _inproj_kernel(x_ref, mod_ref, g1_ref, lbr_ref, w_ref, o_ref, h_sc):
    n = pl.program_id(1)

    @pl.when(n == 0)
    def _():
        x = x_ref[...]
        ms = jnp.mean(x * x, axis=-1, keepdims=True)
        y = x * lax.rsqrt(ms + EPS) * g1_ref[...]
        shift = mod_ref[0, 0:1, :]
        scale = mod_ref[0, 1:2, :]
        h_sc[...] = (y * (1.0 + scale) + shift).astype(BF16)

    def proj():
        return jnp.dot(h_sc[...], w_ref[...], preferred_element_type=F32)

    @pl.when((n == S_Q) | (n == S_OG))
    def _():
        o_ref[...] = _silu(proj()).astype(BF16)

    @pl.when((n == S_GF) | (n == S_GB))
    def _():
        l0 = lbr_ref[0]
        l1 = lbr_ref[1]
        m = jnp.maximum(l0, l1)
        e0 = jnp.exp(l0 - m)
        lb2 = e0 / (e0 + jnp.exp(l1 - m))
        lb = jnp.where(n == S_GB, lb2[1:2, :], lb2[0:1, :])
        o_ref[...] = jnp.log2(0.5 * (1.0 + lb) + (0.5 * (1.0 - lb)) * jnp.tanh(0.5 * proj())).astype(BF16)

    @pl.when((n == S_V) | (n == S_CB) | (n == S_CC) | (n == S_CX))
    def _():
        o_ref[...] = proj().astype(BF16)

    @pl.when(n >= S_GA)
    def _():
        o_ref[...] = _sigmoid(proj()).astype(BF16)


def _inproj(x2, mod3, g1, lbr, w_in, seq_len, tm):
    t = x2.shape[0]
    tiles_per_seq = seq_len // tm
    return pl.pallas_call(
        _inproj_kernel,
        grid=(t // tm, N_SEG),
        in_specs=[pl.BlockSpec((tm, D), lambda i, n: (i, 0)),
                  pl.BlockSpec((1, 6, D), lambda i, n: (i // tiles_per_seq, 0, 0)),
                  pl.BlockSpec((1, D), lambda i, n: (0, 0)),
                  pl.BlockSpec((2, 2, D), lambda i, n: (0, 0, 0)),
                  pl.BlockSpec((D, D), lambda i, n: (0, n))],
        out_specs=pl.BlockSpec((None, tm, D), lambda i, n: (n, i, 0)),
        out_shape=jax.ShapeDtypeStruct((N_SEG, t, D), BF16),
        scratch_shapes=[pltpu.VMEM((tm, D), BF16)],
        compiler_params=_cparams(("arbitrary", "arbitrary")),
        name="inproj",
    )(x2, mod3, g1, lbr, w_in)


def _hgrn_kernel(*refs, reverse, final, n_chunks):
    if final:
        q_ref, g_ref, v_ref, og_ref, ob_ref, nw_ref, o_ref, st_sc = refs
    else:
        q_ref, g_ref, v_ref, o_ref, st_sc = refs

    @pl.when(pl.program_id(1) == 0)
    def _():
        st_sc[...] = jnp.zeros_like(st_sc)

    def row_of(p):
        return (CH - 1 - p) if reverse else p

    ti = lax.broadcasted_iota(jnp.int32, (CH, CH), 0)
    si = lax.broadcasted_iota(jnp.int32, (CH, CH), 1)
    if reverse:
        pt, ps = CH - 1 - ti, CH - 1 - si
    else:
        pt, ps = ti, si
    causal = ps <= pt
    tri = causal.astype(BF16)
    m1 = ((pt >= 16) & (pt < 32) & (ps < 16)) | ((pt >= 48) & (ps >= 32) & (ps < 48))
    m2 = causal & ((pt // 16) == (ps // 16))

    ri = lax.broadcasted_iota(jnp.int32, (CH, D), 0)
    pr = (CH - 1 - ri) if reverse else ri
    blk = pr // 16

    def nt(a, bb):
        return lax.dot_general(a, bb, (((1,), (1,)), ((), ())), preferred_element_type=F32)

    def seg(x, p0, p1):
        return x[CH - p1:CH - p0] if reverse else x[p0:p1]

    def place(pieces):
        pieces = sorted(pieces, key=lambda t: -t[0] if reverse else t[0])
        out, pos = [], 0
        for p0, p1, x in pieces:
            lo = (CH - p1) if reverse else p0
            if lo > pos:
                out.append(jnp.zeros((lo - pos, D), BF16))
            out.append(x)
            pos = lo + (p1 - p0)
        if pos < CH:
            out.append(jnp.zeros((CH - pos, D), BF16))
        return jnp.concatenate(out, axis=0)

    def chunk_group(c2, carry):
        staged = [stage_one(c2 * UNROLL + u) for u in range(UNROLL)]
        for ops in staged:
            stage_two(*ops)
        return carry

    def stage_one(c):
        cc = (n_chunks - 1 - c) if reverse else c
        rows = pl.ds(pl.multiple_of(cc * CH, CH), CH)
        g16 = g_ref[rows, :]
        b = jnp.dot(tri, g16, preferred_element_type=F32)
        q = q_ref[rows, :].astype(F32)
        kk = 1.0 - jnp.exp2(g16.astype(F32))
        v = v_ref[rows, :]

        def bro(p):
            return b[row_of(p):row_of(p) + 1, :]

        def qside(p0, p1, ref):
            return (p0, p1, (seg(q, p0, p1) * jnp.exp2(seg(b, p0, p1) - ref)).astype(BF16))

        def kside(p0, p1, ref):
            return (p0, p1, (seg(kk, p0, p1) * jnp.exp2(ref - seg(b, p0, p1))).astype(BF16))

        r0 = bro(31)
        qa = place([qside(32, 64, r0)])
        ka = place([kside(0, 32, r0)])
        r1a, r1b = bro(15), bro(47)
        qb = place([qside(16, 32, r1a), qside(48, 64, r1b)])
        kb = place([kside(0, 16, r1a), kside(32, 48, r1b)])
        mid = jnp.where(blk == 0, bro(7), jnp.where(blk == 1, bro(23), jnp.where(blk == 2, bro(39), bro(55))))
        qc = (q * jnp.exp2(jnp.minimum(b - mid, EXP2_CLAMP))).astype(BF16)
        kc = (kk * jnp.exp2(jnp.minimum(mid - b, EXP2_CLAMP))).astype(BF16)
        bl = bro(CH - 1)
        qe = (q * jnp.exp2(b)).astype(BF16)
        kd = (kk * jnp.exp2(bl - b)).astype(BF16)
        dec = jnp.exp2(bl)
        return rows, v, qa, ka, qb, kb, qc, kc, qe, kd, dec

    def stage_two(rows, v, qa, ka, qb, kb, qc, kc, qe, kd, dec):
        heads = [slice(h * HD, (h + 1) * HD) for h in range(HEADS)]
        att = [(nt(qa[:, c], ka[:, c]) + jnp.where(m1, nt(qb[:, c], kb[:, c]), 0.0)
                + jnp.where(m2, nt(qc[:, c], kc[:, c]), 0.0)).astype(BF16) for c in heads]
        st = [st_sc[h] for h in range(HEADS)]
        inter = [nt(qe[:, c], st[h].astype(BF16)) for h, c in enumerate(heads)]
        upd = [lax.dot_general(v[:, c], kd[:, c], (((0,), (0,)), ((), ())), preferred_element_type=F32)
               for c in heads]
        intra = [jnp.dot(att[h], v[:, c], preferred_element_type=F32) for h, c in enumerate(heads)]
        for h, c in enumerate(heads):
            st_sc[h] = st[h] * dec[:, c] + upd[h]
            o = intra[h] + inter[h]
            if final:
                o = o + ob_ref[rows, c]
                o = o * lax.rsqrt(jnp.mean(o * o, axis=-1, keepdims=True) + EPS)
                o = o * nw_ref[...] * og_ref[rows, c].astype(F32)
                o_ref[rows, c] = o.astype(o_ref.dtype)
            else:
                o_ref[rows, c] = o

    lax.fori_loop(0, n_chunks // UNROLL, chunk_group, 0)


def _hgrn(proj, o_bw, hg_norm_w, seq_len, ct, reverse, final):
    t = proj.shape[1]
    nt_ = seq_len // ct
    nb = t // seq_len
    assert ct % (CH * UNROLL) == 0, (ct, CH, UNROLL)

    def tile(b, j):
        return b * nt_ + ((nt_ - 1 - j) if reverse else j)

    def pspec(slot):
        return pl.BlockSpec((None, ct, D), lambda b, j: (slot, tile(b, j), 0))

    in_specs = [pspec(S_Q), pspec(S_GB if reverse else S_GF), pspec(S_V)]
    args = [proj, proj, proj]
    if final:
        in_specs += [pspec(S_OG), pl.BlockSpec((ct, D), lambda b, j: (tile(b, j), 0)),
                     pl.BlockSpec((1, HD), lambda b, j: (0, 0))]
        args += [proj, o_bw, hg_norm_w]
    return pl.pallas_call(
        functools.partial(_hgrn_kernel, reverse=reverse, final=final, n_chunks=ct // CH),
        grid=(nb, nt_),
        in_specs=in_specs,
        out_specs=pl.BlockSpec((ct, D), lambda b, j: (tile(b, j), 0)),
        out_shape=jax.ShapeDtypeStruct((t, D), BF16 if final else F32),
        scratch_shapes=[pltpu.VMEM((HEADS, HD, HD), F32)],
        compiler_params=_cparams(("arbitrary", "arbitrary")),
        name="hgrn_final" if final else "hgrn_bw",
    )(*args)


def _mix_kernel(og_ref, cb_ref, cc_ref, cx_ref, ccp_ref, cxp_ref, ccn_ref, cxn_ref, ga_ref, gb_ref, x_ref,
                mod_ref, cw_ref, wa_ref, wb_ref, wo_ref, g2_ref, s1_ref, s3_ref, s2_ref,
                xsh_ref, h2_ref, *, tiles_per_seq, tm):
    i = pl.program_id(0)
    pos = i % tiles_per_seq
    u = cc_ref[...].astype(F32) * cx_ref[...].astype(F32)
    prev_row = jnp.where(pos == 0, 0.0, ccp_ref[15:16, :].astype(F32) * cxp_ref[15:16, :].astype(F32))
    next_row = jnp.where(pos == tiles_per_seq - 1, 0.0,
                         ccn_ref[0:1, :].astype(F32) * cxn_ref[0:1, :].astype(F32))
    ri = lax.broadcasted_iota(jnp.int32, (tm, D), 0)
    u_prev = jnp.where(ri == 0, prev_row, pltpu.roll(u, 1, axis=0))
    u_next = jnp.where(ri == tm - 1, next_row, pltpu.roll(u, tm - 1, axis=0))
    conv = u_prev * cw_ref[0:1, :] + u * cw_ref[1:2, :] + u_next * cw_ref[2:3, :]
    yb = jnp.dot((cb_ref[...].astype(F32) * conv).astype(BF16), wb_ref[...], preferred_element_type=F32)
    ya = jnp.dot(og_ref[...], wa_ref[...], preferred_element_type=F32)
    merged = ga_ref[...].astype(F32) * ya + gb_ref[...].astype(F32) * yb
    mix = jnp.dot(merged.astype(BF16), wo_ref[...], preferred_element_type=F32)
    gate1 = mod_ref[0, 2:3, :]
    shift2 = mod_ref[0, 3:4, :]
    scale2 = mod_ref[0, 4:5, :]
    gate2 = mod_ref[0, 5:6, :]
    x1 = x_ref[...] + gate1 * mix
    ms = jnp.mean(x1 * x1, axis=-1, keepdims=True)
    h2 = (x1 * lax.rsqrt(ms + EPS) * g2_ref[...]) * (1.0 + scale2) + shift2
    h2_ref[...] = h2
    hb = h2.astype(BF16)
    a = _silu(jnp.dot(hb, s1_ref[...], preferred_element_type=F32)) * jnp.dot(
        hb, s3_ref[...], preferred_element_type=F32)
    shared = jnp.dot(a.astype(BF16), s2_ref[...], preferred_element_type=F32)
    xsh_ref[...] = x1 + gate2 * shared


def _mix(proj, og, x2, mod3, conv_w, w_o_hg, w_o_conv, w_out, g2, s1, s3, s2, seq_len, tm):
    t = x2.shape[0]
    tiles_per_seq = seq_len // tm
    hb = tm // 16
    nhb = t // 16

    def pspec(slot):
        return pl.BlockSpec((None, tm, D), lambda i: (slot, i, 0))

    def halo(slot, nxt):
        if nxt:
            return pl.BlockSpec((None, 16, D), lambda i: (slot, jnp.minimum((i + 1) * hb, nhb - 1), 0))
        return pl.BlockSpec((None, 16, D), lambda i: (slot, jnp.maximum(i * hb - 1, 0), 0))

    def full(shape):
        return pl.BlockSpec(shape, lambda i: (0,) * len(shape))

    in_specs = [pl.BlockSpec((tm, D), lambda i: (i, 0)),
                pspec(S_CB), pspec(S_CC), pspec(S_CX),
                halo(S_CC, False), halo(S_CX, False), halo(S_CC, True), halo(S_CX, True),
                pspec(S_GA), pspec(S_GBM),
                pl.BlockSpec((tm, D), lambda i: (i, 0)),
                pl.BlockSpec((1, 6, D), lambda i: (i // tiles_per_seq, 0, 0)),
                full((3, D)), full((D, D)), full((D, D)), full((D, D)), full((1, D)),
                full((D, HID)), full((D, HID)), full((HID, D))]
    return pl.pallas_call(
        functools.partial(_mix_kernel, tiles_per_seq=tiles_per_seq, tm=tm),
        grid=(t // tm,),
        in_specs=in_specs,
        out_specs=[pl.BlockSpec((tm, D), lambda i: (i, 0)), pl.BlockSpec((tm, D), lambda i: (i, 0))],
        out_shape=[jax.ShapeDtypeStruct((t, D), F32), jax.ShapeDtypeStruct((t, D), F32)],
        compiler_params=_cparams(("arbitrary",)),
        name="mix",
    )(og, proj, proj, proj, proj, proj, proj, proj, proj, proj, x2, mod3, conv_w, w_o_hg, w_o_conv, w_out,
      g2, s1, s3, s2)


def _route_kernel(h_ref, rwt_ref, bias_ref, ustrict_ref, eid_ref, rank_ref, wt_ref, cnt_ref, *, tm):
    @pl.when(pl.program_id(0) == 0)
    def _():
        cnt_ref[...] = jnp.zeros_like(cnt_ref)

    logits = lax.dot_general(rwt_ref[...], h_ref[...], (((1,), (1,)), ((), ())),
                             precision=HIGHEST, preferred_element_type=F32)
    scores = _sigmoid(logits)
    sel = scores + bias_ref[:, 0:1]

    ief = lax.broadcasted_iota(jnp.int32, (N_EXP, tm), 0).astype(F32)
    sel3 = sel.reshape(N_GRP, EXP_PER_GRP, tm)
    i3 = lax.broadcasted_iota(jnp.int32, (N_GRP, EXP_PER_GRP, tm), 1).astype(F32)
    top1 = jnp.max(sel3, axis=1, keepdims=True)
    first = jnp.min(jnp.where(sel3 == top1, i3, float(EXP_PER_GRP)), axis=1, keepdims=True)
    top2 = jnp.max(jnp.where(i3 == first, -jnp.inf, sel3), axis=1, keepdims=True)
    grp = jnp.broadcast_to(top1 + top2, (N_GRP, EXP_PER_GRP, tm)).reshape(N_EXP, tm)
    ig = jnp.floor(ief * (1.0 / EXP_PER_GRP))
    beaten = jnp.zeros((N_EXP, tm), F32)
    for j in range(N_GRP):
        gj = grp[j * EXP_PER_GRP:j * EXP_PER_GRP + 1, :]
        beaten = beaten + jnp.where((gj > grp) | ((gj == grp) & (float(j) < ig)), 1.0, 0.0)
    cand = jnp.where(beaten < float(TOPK_GRP), sel, -jnp.inf)
    chosen = jnp.zeros((N_EXP, tm), F32)
    for _ in range(TOP_K):
        mx = jnp.max(cand, axis=0, keepdims=True)
        idx = jnp.min(jnp.where(cand == mx, ief, float(N_EXP)), axis=0, keepdims=True)
        hit = ief == idx
        chosen = jnp.where(hit, 1.0, chosen)
        cand = jnp.where(hit, -jnp.inf, cand)

    w = scores * chosen
    w = w / jnp.sum(w, axis=0, keepdims=True) * ROUTED_SCALE
    chosen_b = chosen.astype(BF16)
    rank = jnp.dot(chosen_b, ustrict_ref[...], preferred_element_type=F32) + cnt_ref[:, 0:1]
    cnt_ref[...] = cnt_ref[...] + jnp.sum(chosen, axis=1, keepdims=True)
    lstrict = (lax.broadcasted_iota(jnp.int32, (N_EXP, N_EXP), 1)
               < lax.broadcasted_iota(jnp.int32, (N_EXP, N_EXP), 0)).astype(BF16)
    slotpos = jnp.dot(lstrict, chosen_b, preferred_element_type=F32)
    for k in range(TOP_K):
        mk = (chosen > 0.5) & (slotpos == float(k))
        eid_ref[k:k + 1, :] = jnp.sum(jnp.where(mk, ief, 0.0), axis=0, keepdims=True).astype(jnp.int32)
        rank_ref[k:k + 1, :] = jnp.sum(jnp.where(mk, rank, 0.0), axis=0, keepdims=True).astype(jnp.int32)
        wt_ref[k:k + 1, :] = jnp.sum(jnp.where(mk, w, 0.0), axis=0, keepdims=True)


def _route(h2, router_wt, router_bias, tm):
    t = h2.shape[0]
    ustrict = jnp.triu(jnp.ones((tm, tm), BF16), 1)
    bias = jnp.broadcast_to(router_bias.reshape(N_EXP, 1), (N_EXP, LANES))
    kt = lambda i: (0, i)
    return pl.pallas_call(
        functools.partial(_route_kernel, tm=tm),
        grid=(t // tm,),
        in_specs=[pl.BlockSpec((tm, D), lambda i: (i, 0)),
                  pl.BlockSpec((N_EXP, D), lambda i: (0, 0)),
                  pl.BlockSpec((N_EXP, LANES), lambda i: (0, 0)),
                  pl.BlockSpec((tm, tm), lambda i: (0, 0))],
        out_specs=[pl.BlockSpec((TOP_K, tm), kt), pl.BlockSpec((TOP_K, tm), kt),
                   pl.BlockSpec((TOP_K, tm), kt), pl.BlockSpec((N_EXP, LANES), lambda i: (0, 0))],
        out_shape=[jax.ShapeDtypeStruct((TOP_K, t), jnp.int32), jax.ShapeDtypeStruct((TOP_K, t), jnp.int32),
                   jax.ShapeDtypeStruct((TOP_K, t), F32), jax.ShapeDtypeStruct((N_EXP, LANES), F32)],
        compiler_params=_cparams(("arbitrary",)),
        name="route",
    )(h2, router_wt, bias, ustrict)


def _pack_words(x):
    return [pltpu.pack_elementwise([x[:, j * LANES:(j + 1) * LANES], x[:, HALF + j * LANES:HALF + (j + 1) * LANES]],
                                   packed_dtype=BF16) for j in range(PACK_ROWS)]


def _unpack_words(w):
    return tuple(pltpu.unpack_elementwise(w, index=i, packed_dtype=BF16, unpacked_dtype=F32) for i in range(2))


def _token_rows(ref, row):
    return ref.at[pl.ds(pl.multiple_of(row * PACK_ROWS, PACK_ROWS), PACK_ROWS), :]


def _strided(ref, start, m):
    return ref[pl.ds(start, m, stride=PACK_ROWS), :]


def _dispatch_kernel(dest_ref, h_ref, xs_ref, slab, sem, *, tm):
    for j, wj in enumerate(_pack_words(h_ref[...])):
        slab[pl.ds(j, tm, stride=PACK_ROWS), :] = wj

    def send(t, carry):
        src = _token_rows(slab, t)
        for k in range(TOP_K):
            pltpu.make_async_copy(src, _token_rows(xs_ref, dest_ref[k, t]), sem).start(priority=k % 2)
        return carry

    lax.fori_loop(0, tm, send, 0)
    for _ in range(TOP_K):
        pltpu.make_async_copy(slab, xs_ref.at[pl.ds(0, tm * PACK_ROWS), :], sem).wait()


def _dispatch(h2, dest, tm):
    t = h2.shape[0]
    return pl.pallas_call(
        functools.partial(_dispatch_kernel, tm=tm),
        grid=(t // tm,),
        in_specs=[pl.BlockSpec((TOP_K, tm), lambda i: (0, i), memory_space=pltpu.SMEM),
                  pl.BlockSpec((tm, D), lambda i: (i, 0))],
        out_specs=pl.BlockSpec(memory_space=pl.ANY),
        out_shape=jax.ShapeDtypeStruct((t * TOP_K * PACK_ROWS, LANES), jnp.uint32),
        scratch_shapes=[pltpu.VMEM((tm * PACK_ROWS, LANES), jnp.uint32), pltpu.SemaphoreType.DMA],
        compiler_params=_cparams(("arbitrary",)),
        name="dispatch",
    )(dest, h2)


def _gmm_kernel(blk_ref, eid_ref, x_ref, w1_ref, w3_ref, w2_ref, o_ref, *, bm):
    sub = bm // GMM_SPLIT
    for s in range(GMM_SPLIT):
        base = s * sub * PACK_ROWS
        parts = [_unpack_words(_strided(x_ref, base + j, sub)) for j in range(PACK_ROWS)]
        x = jnp.concatenate([p[0] for p in parts] + [p[1] for p in parts], axis=1).astype(BF16)
        a = _silu(jnp.dot(x, w1_ref[...], preferred_element_type=F32)) * jnp.dot(
            x, w3_ref[...], preferred_element_type=F32)
        y = jnp.dot(a.astype(BF16), w2_ref[...], preferred_element_type=F32)
        for j, wj in enumerate(_pack_words(y)):
            o_ref[pl.ds(base + j, sub, stride=PACK_ROWS), :] = wj


def _gmm(xs, blk, item_e, w1, w3, w2, bm):
    n_items = blk.shape[0]
    grid_spec = pltpu.PrefetchScalarGridSpec(
        num_scalar_prefetch=2,
        grid=(n_items,),
        in_specs=[pl.BlockSpec((bm * PACK_ROWS, LANES), lambda w, blk, eid: (blk[w], 0)),
                  pl.BlockSpec((None, D, HID), lambda w, blk, eid: (eid[w], 0, 0)),
                  pl.BlockSpec((None, D, HID), lambda w, blk, eid: (eid[w], 0, 0)),
                  pl.BlockSpec((None, HID, D), lambda w, blk, eid: (eid[w], 0, 0))],
        out_specs=pl.BlockSpec((bm * PACK_ROWS, LANES), lambda w, blk, eid: (w, 0)),
    )
    return pl.pallas_call(
        functools.partial(_gmm_kernel, bm=bm),
        grid_spec=grid_spec,
        out_shape=jax.ShapeDtypeStruct((n_items * bm * PACK_ROWS, LANES), xs.dtype),
        compiler_params=_cparams(("arbitrary",)),
        name="gmm",
    )(blk, item_e, xs, w1, w3, w2)


def _gmm_metadata(counts, n_rows, bm):
    n_blk = n_rows // bm
    n_items = n_blk + N_EXP - 1
    ends = jnp.cumsum(counts)
    starts = ends - counts
    first_blk = starts // bm
    last_blk = jnp.maximum(ends - 1, 0) // bm
    per_e = jnp.where(counts > 0, last_blk - first_blk + 1, 0)
    item_end = jnp.cumsum(per_e)
    item_start = item_end - per_e
    total = item_end[-1]
    w = jnp.minimum(jnp.arange(n_items, dtype=jnp.int32), total - 1)
    e = jnp.sum((item_end[None, :] <= w[:, None]).astype(jnp.int32), axis=1)
    blk = (_lookup(first_blk, e) + w - _lookup(item_start, e)).astype(jnp.int32)
    return blk, e, starts, item_start - first_blk


def _lookup(table, idx):
    hit = idx[..., None] == jnp.arange(table.shape[0], dtype=jnp.int32)
    return jnp.sum(jnp.where(hit, table.astype(jnp.int32), 0), axis=-1)


def _combine_kernel(dest_ref, next_ref, wt_ref, xsh_ref, mod_ref, fg_ref, ys_ref, o_ref, ybuf, sem, *, tm, n_steps):
    i = pl.program_id(0)
    slot = i % 2
    rows_per_slot = TOP_K * tm

    def fetch(idx_ref, into):
        def body(t, carry):
            for k in range(TOP_K):
                pltpu.make_async_copy(_token_rows(ys_ref, idx_ref[k, t]),
                                      _token_rows(ybuf, into * rows_per_slot + k * tm + t),
                                      sem.at[into]).start(priority=k % 2)
            return carry
        lax.fori_loop(0, tm, body, 0)

    @pl.when(i == 0)
    def _():
        fetch(dest_ref, slot)

    @pl.when(i + 1 < n_steps)
    def _():
        fetch(next_ref, 1 - slot)

    base = pl.multiple_of(slot * (rows_per_slot * PACK_ROWS), rows_per_slot * PACK_ROWS)
    half = ybuf.at[pl.ds(base, rows_per_slot * PACK_ROWS), :]
    pltpu.make_async_copy(ys_ref.at[pl.ds(0, rows_per_slot * PACK_ROWS), :], half, sem.at[slot]).wait()

    lo = [jnp.zeros((tm, LANES), F32) for _ in range(PACK_ROWS)]
    hi = [jnp.zeros((tm, LANES), F32) for _ in range(PACK_ROWS)]
    for k in range(TOP_K):
        wk = wt_ref[:, k:k + 1]
        for j in range(PACK_ROWS):
            l, h = _unpack_words(ybuf[pl.ds(base + k * tm * PACK_ROWS + j, tm, stride=PACK_ROWS), :])
            lo[j] = lo[j] + wk * l
            hi[j] = hi[j] + wk * h
    routed = jnp.concatenate(lo + hi, axis=1)
    gate2 = mod_ref[0, 5:6, :]
    x = xsh_ref[...] + gate2 * routed
    ms = jnp.mean(x * x, axis=-1, keepdims=True)
    o_ref[...] = x * lax.rsqrt(ms + EPS) * fg_ref[...]


def _combine(ys, src, wts_t, xsh, mod3, final_g, seq_len, tm):
    t = xsh.shape[0]
    tiles_per_seq = seq_len // tm
    n_steps = t // tm
    return pl.pallas_call(
        functools.partial(_combine_kernel, tm=tm, n_steps=n_steps),
        grid=(n_steps,),
        in_specs=[pl.BlockSpec((TOP_K, tm), lambda i: (0, i), memory_space=pltpu.SMEM),
                  pl.BlockSpec((TOP_K, tm), lambda i: (0, jnp.minimum(i + 1, n_steps - 1)),
                               memory_space=pltpu.SMEM),
                  pl.BlockSpec((tm, TOP_K), lambda i: (i, 0)),
                  pl.BlockSpec((tm, D), lambda i: (i, 0)),
                  pl.BlockSpec((1, 6, D), lambda i: (i // tiles_per_seq, 0, 0)),
                  pl.BlockSpec((1, D), lambda i: (0, 0)),
                  pl.BlockSpec(memory_space=pl.ANY)],
        out_specs=pl.BlockSpec((tm, D), lambda i: (i, 0)),
        out_shape=jax.ShapeDtypeStruct((t, D), F32),
        scratch_shapes=[pltpu.VMEM((2 * TOP_K * tm * PACK_ROWS, LANES), jnp.uint32),
                        pltpu.SemaphoreType.DMA((2,))],
        compiler_params=_cparams(("arbitrary",)),
        name="combine",
    )(src, src, wts_t, xsh, mod3, final_g, ys)


def _pick(n, cap):
    t = cap
    while n % t:
        t //= 2
    return t


def _encode(x, mod, p):
    nb, seq_len, _ = x.shape
    t = nb * seq_len
    x2 = x.reshape(t, D)
    mod3 = mod.reshape(nb, 6, D)

    proj = _inproj(x2, mod3, p["norm1_g"], p["lower_bounds"], p["w_in"], seq_len, _pick(seq_len, 2048))
    ct = _pick(seq_len, 1024)
    o_bw = _hgrn(proj, None, None, seq_len, ct, reverse=True, final=False)
    og = _hgrn(proj, o_bw, p["hg_norm_w"], seq_len, ct, reverse=False, final=True)
    xsh, h2 = _mix(proj, og, x2, mod3, p["conv_w"], p["w_o_hg"], p["w_o_conv"], p["w_out"], p["norm2_g"],
                   p["sh_w1"], p["sh_w3"], p["sh_w2"], seq_len, _pick(seq_len, 512))

    eid, rank, wts, cnt = _route(h2, p["router_wt"], p["router_bias"], _pick(t, 1024))
    counts = cnt[:, 0].astype(jnp.int32)
    bm = _pick(t * TOP_K, GMM_ROWS)
    blk, item_e, starts, item_shift = _gmm_metadata(counts, t * TOP_K, bm)
    dest = _lookup(starts, eid) + rank
    xs = _dispatch(h2, dest, _pick(t, 512))
    ys = _gmm(xs, blk, item_e, p["exp_w1"], p["exp_w3"], p["exp_w2"], bm)
    src = dest + _lookup(item_shift, eid) * bm
    out = _combine(ys, src, wts.T, xsh, mod3, p["final_g"], seq_len, _pick(seq_len, 256))
    return out.reshape(nb, seq_len, D)


def kernel(x_prompt, x_sample, c_prompt, c_sample, w_ada, b_ada, norm1_g, w_in, lower_bounds, hg_norm_w,
           w_o_hg, conv_w, w_o_conv, w_out, norm2_g, router_w, router_bias, exp_w1, exp_w3, exp_w2,
           sh_w1, sh_w3, sh_w2, final_g):
    p = {
        "norm1_g": norm1_g[0].reshape(1, D),
        "lower_bounds": lower_bounds.astype(F32),
        "w_in": w_in[0].astype(BF16),
        "hg_norm_w": hg_norm_w[0].reshape(1, HD),
        "w_o_hg": w_o_hg[0].astype(BF16),
        "conv_w": conv_w[0],
        "w_o_conv": w_o_conv[0].astype(BF16),
        "w_out": w_out[0].astype(BF16),
        "norm2_g": norm2_g[0].reshape(1, D),
        "router_wt": router_w[0].T,
        "router_bias": router_bias[0],
        "exp_w1": exp_w1[0].astype(BF16),
        "exp_w3": exp_w3[0].astype(BF16),
        "exp_w2": exp_w2[0].astype(BF16),
        "sh_w1": sh_w1[0].astype(BF16),
        "sh_w3": sh_w3[0].astype(BF16),
        "sh_w2": sh_w2[0].astype(BF16),
        "final_g": final_g.reshape(1, D),
    }
    nbp = c_prompt.shape[0]
    mod = _ada(jnp.concatenate([c_prompt, c_sample], axis=0), w_ada[0], b_ada[0])
    y_prompt = _encode(x_prompt, mod[:nbp], p)
    y_sample = _encode(x_sample, mod[nbp:], p)
    return (y_prompt, y_sample)
```
